```python
import math
import jax, jax.numpy as jnp
from jax import lax
import numpy as np

D_MODEL = 2048
BATCH = 2
SEQ = 4096
DEPTH = 2
DEC_BATCH = 128
DEC_SEQ = 1
PAST_LEN = 8192
PAGE_SIZE = 128

N_BRANCH = 4
W_BRANCH = D_MODEL // 2
QK_NOPE = 128
QK_ROPE = 64
V_HEAD = 128
N_HEADS = W_BRANCH // V_HEAD
Q_LORA = D_MODEL // 4
KV_LORA = 512
ROPE_THETA = 10000.0
Q_BLOCK = 128
MLA_SCALE = (QK_NOPE + QK_ROPE) ** -0.5
S5_GC = 16
S5_GROUPS = W_BRANCH // S5_GC
S5_STATE = 64
DT_MIN = 0.001
DT_MAX = 0.1
POOL_WINDOWS = (2, 4, 8, 16)
POOL_GROUPS = 4
POOL_GW = W_BRANCH // POOL_GROUPS
POOL_BUF = max(POOL_WINDOWS) - 1
CHUNK = 128
GMLP_GROUPS = 4
GMLP_GW = W_BRANCH // GMLP_GROUPS
NORM_EPS = 1e-6
IN_SPLITS = (Q_LORA, KV_LORA, QK_ROPE, W_BRANCH,
             W_BRANCH, W_BRANCH,
             W_BRANCH, W_BRANCH,
             W_BRANCH, W_BRANCH, W_BRANCH,
             N_BRANCH * D_MODEL)
D_IN = sum(IN_SPLITS)

kernel_name = 'hybrid_mla_s5_pool_gmlp_decode_step'


def rmsnorm(x, g):
    xf = x.astype(jnp.float32)
    y = xf * lax.rsqrt(jnp.mean(xf * xf, axis=-1, keepdims=True) + NORM_EPS)
    return (y * g.astype(jnp.float32)).astype(x.dtype)


def layernorm(x, g, b):
    xf = x.astype(jnp.float32)
    mu = jnp.mean(xf, axis=-1, keepdims=True)
    xc = xf - mu
    y = xc * lax.rsqrt(jnp.mean(xc * xc, axis=-1, keepdims=True) + NORM_EPS)
    return (y * g.astype(jnp.float32) + b.astype(jnp.float32)).astype(x.dtype)


def rope(x, pos):
    half = QK_ROPE // 2
    freqs = ROPE_THETA ** (-jnp.arange(half, dtype=jnp.float32) / half)
    ang = pos.astype(jnp.float32)[:, None] * freqs[None, :]
    cos = jnp.cos(ang)[None, :, None, :].astype(x.dtype)
    sin = jnp.sin(ang)[None, :, None, :].astype(x.dtype)
    x1, x2 = x[..., :half], x[..., half:]
    return jnp.concatenate([x1 * cos - x2 * sin, x1 * sin + x2 * cos], axis=-1)


def mla_scores(q_lat, q_pe, k_lat, k_pe):
    s = jnp.einsum('bqhr,bkr->bhqk', q_lat, k_lat) + jnp.einsum('bqhe,bke->bhqk', q_pe, k_pe)
    return s.astype(jnp.float32) * MLA_SCALE


def mla_prompt_attend(q_lat, q_pe, k_lat, k_pe, pos):
    B, L, H, R = q_lat.shape
    nb = L // Q_BLOCK
    qb = q_lat.reshape(B, nb, Q_BLOCK, H, R).swapaxes(0, 1)
    pb = q_pe.reshape(B, nb, Q_BLOCK, H, QK_ROPE).swapaxes(0, 1)
    posb = pos.reshape(nb, Q_BLOCK)

    def block(args):
        ql, qp, qpos = args
        s = mla_scores(ql, qp, k_lat, k_pe)
        mask = pos[None, :] <= qpos[:, None]
        s = jnp.where(mask[None, None], s, -jnp.inf)
        w = jax.nn.softmax(s, axis=-1).astype(k_lat.dtype)
        return jnp.einsum('bhqk,bkr->bqhr', w, k_lat)

    o = lax.map(block, (qb, pb, posb))
    return o.swapaxes(0, 1).reshape(B, L, H, R)


def mla_sample_attend(q_lat, q_pe, past_lat, past_pe, k_lat, k_pe, pos):
    n_past = past_lat.shape[1]
    s_past = mla_scores(q_lat, q_pe, past_lat, past_pe)
    s_new = mla_scores(q_lat, q_pe, k_lat, k_pe)
    mask = pos[None, :] <= pos[:, None]
    s_new = jnp.where(mask[None, None], s_new, -jnp.inf)
    w = jax.nn.softmax(jnp.concatenate([s_past, s_new], axis=-1), axis=-1).astype(k_lat.dtype)
    return (jnp.einsum('bhqk,bkr->bqhr', w[..., :n_past], past_lat)
            + jnp.einsum('bhqk,bkr->bqhr', w[..., n_past:], k_lat))


def s5_scan(u, h0_re, h0_im, lam_re, lam_im, log_dt, b_re, b_im, c_re, c_im, d):
    B, L, _ = u.shape
    ug = u.reshape(B, L, S5_GROUPS, S5_GC)
    dt = jnp.exp(log_dt)[:, None]
    ld_re, ld_im = lam_re * dt, lam_im * dt
    mag = jnp.exp(ld_re)
    a_re, a_im = mag * jnp.cos(ld_im), mag * jnp.sin(ld_im)
    den = lam_re * lam_re + lam_im * lam_im
    num_re, num_im = a_re - 1.0, a_im
    coef_re = (num_re * lam_re + num_im * lam_im) / den
    coef_im = (num_im * lam_re - num_re * lam_im) / den
    bb_re = coef_re[..., None] * b_re - coef_im[..., None] * b_im
    bb_im = coef_re[..., None] * b_im + coef_im[..., None] * b_re
    bu_re = jnp.einsum('blgc,gpc->blgp', ug, bb_re)
    bu_im = jnp.einsum('blgc,gpc->blgp', ug, bb_im)
    bu_re = bu_re.at[:, 0].add(a_re * h0_re - a_im * h0_im)
    bu_im = bu_im.at[:, 0].add(a_re * h0_im + a_im * h0_re)
    ar = jnp.broadcast_to(a_re, bu_re.shape)
    ai = jnp.broadcast_to(a_im, bu_im.shape)

    def combine(e1, e2):
        a1r, a1i, b1r, b1i = e1
        a2r, a2i, b2r, b2i = e2
        return (a2r * a1r - a2i * a1i, a2r * a1i + a2i * a1r,
                a2r * b1r - a2i * b1i + b2r, a2r * b1i + a2i * b1r + b2i)

    _, _, h_re, h_im = lax.associative_scan(combine, (ar, ai, bu_re, bu_im), axis=1)
    y = (jnp.einsum('gcp,blgp->blgc', c_re, h_re) - jnp.einsum('gcp,blgp->blgc', c_im, h_im)
         + d * ug)
    return y.reshape(B, L, W_BRANCH), h_re[:, -1], h_im[:, -1]


def pool_mix(u_ext, pos_ext, n_new, pool_w, pool_scale):
    B, Le, _ = u_ext.shape
    ug = u_ext.reshape(B, Le, POOL_GROUPS, POOL_GW)
    c = jnp.cumsum(ug.astype(jnp.float32), axis=1)
    outs = []
    for g, w in enumerate(POOL_WINDOWS):
        cg = jnp.pad(c[:, :, g], ((0, 0), (w, 0), (0, 0)))
        s = cg[:, w:] - cg[:, :-w]
        cnt = jnp.minimum(pos_ext + 1, w).astype(jnp.float32)[None, :, None]
        outs.append(s / cnt - ug[:, :, g].astype(jnp.float32))
    dlt = jnp.stack(outs, axis=2)[:, Le - n_new:].astype(u_ext.dtype)
    y = jnp.einsum('blgi,gio->blgo', dlt, pool_w).reshape(B, n_new, W_BRANCH)
    return y * pool_scale


def chunk_spatial_gate(u, v, ln_g, ln_b, w_s, b_s):
    B, L, _ = u.shape
    vn = layernorm(v, ln_g, ln_b)
    nc = -(-L // CHUNK)
    lp = nc * CHUNK
    vp = jnp.pad(vn, ((0, 0), (0, lp - L), (0, 0))).reshape(B, nc, CHUNK, GMLP_GROUPS, GMLP_GW)
    ws = w_s * jnp.tril(jnp.ones((CHUNK, CHUNK), w_s.dtype))[None]
    mixed = jnp.einsum('gts,bcsgd->bctgd', ws, vp) + jnp.transpose(b_s)[None, None, :, :, None]
    mixed = mixed.reshape(B, lp, W_BRANCH)[:, :L]
    return u * mixed, vn


def hybrid_layer(h, pos, p, past):
    B, L, _ = h.shape
    xn = rmsnorm(h, p['norm'])
    proj = xn @ p['w_in']
    idx = [int(v) for v in np.cumsum(IN_SPLITS)[:-1]]
    (c_q, c_kv, k_pe_raw, g_mla, u_s5, g_s5, u_pool, g_pool,
     u_gm, v_gm, g_gm, g_merge) = jnp.split(proj, idx, axis=-1)

    q = jnp.einsum('blr,rhe->blhe', rmsnorm(c_q, p['q_norm']), p['w_uq'])
    q_pe = rope(q[..., QK_NOPE:], pos)
    q_lat = jnp.einsum('blhd,rhd->blhr', q[..., :QK_NOPE], p['w_uk'])
    k_lat = rmsnorm(c_kv, p['kv_norm'])
    k_pe = rope(k_pe_raw[:, :, None, :], pos)[:, :, 0]
    if past is None:
        o_lat = mla_prompt_attend(q_lat, q_pe, k_lat, k_pe, pos)
    else:
        o_lat = mla_sample_attend(q_lat, q_pe, past['kv_lat'], past['k_pe'], k_lat, k_pe, pos)
    o_mla = jnp.einsum('blhr,rhd->blhd', o_lat, p['w_uv']).reshape(B, L, W_BRANCH)

    if past is None:
        h0_re = jnp.zeros((B, S5_GROUPS, S5_STATE), u_s5.dtype)
        h0_im = jnp.zeros((B, S5_GROUPS, S5_STATE), u_s5.dtype)
    else:
        h0_re, h0_im = past['s5_re'], past['s5_im']
    y_s5, s5_re, s5_im = s5_scan(u_s5, h0_re, h0_im, p['lam_re'], p['lam_im'], p['log_dt'],
                                 p['b_re'], p['b_im'], p['c_re'], p['c_im'], p['d'])
    z = jax.nn.gelu(y_s5)
    o_s5 = z * jax.nn.sigmoid(z @ p['w_glu'] + p['b_glu'])

    if past is None:
        u_ext, pos_ext = u_pool, pos
    else:
        u_ext = jnp.concatenate([past['pool'], u_pool], axis=1)
        pos_ext = jnp.concatenate([pos[0] - POOL_BUF + jnp.arange(POOL_BUF, dtype=pos.dtype), pos])
    o_pool = pool_mix(u_ext, pos_ext, L, p['pool_w'], p['pool_scale'])
    pool_state = u_ext[:, u_ext.shape[1] - POOL_BUF:]

    o_gm, v_rows = chunk_spatial_gate(u_gm, v_gm, p['ln_g'], p['ln_b'], p['w_s'], p['b_s'])

    branches = jnp.stack([o_mla * jax.nn.silu(g_mla), o_s5 * jax.nn.silu(g_s5),
                          o_pool * jax.nn.silu(g_pool), o_gm * jax.nn.silu(g_gm)], axis=2)
    y_br = jnp.einsum('blkw,kwd->blkd', branches, p['w_branch'])
    gates = jax.nn.sigmoid(g_merge.reshape(B, L, N_BRANCH, D_MODEL))
    merged = jnp.sum(gates * y_br, axis=2)
    out = h + merged @ p['w_out']
    return out, (k_lat, k_pe, s5_re, s5_im, pool_state, v_rows)


def setup_inputs(seed: int = 0) -> dict:
    key = jax.random.key(seed)
    ks = iter(jax.random.split(key, 48))

    def normal(shape, scale):
        return jax.random.normal(next(ks), shape, jnp.float32) * scale

    n_pages = PAST_LEN // PAGE_SIZE
    n_used = DEC_BATCH * n_pages
    n_pool = n_used + n_used // 4
    x_prompt = normal((BATCH, SEQ, D_MODEL), 1.0)
    x_sample = normal((DEC_BATCH, DEC_SEQ, D_MODEL), 1.0)
    cache_kv_latent = normal((DEPTH, n_pool, PAGE_SIZE, KV_LORA), 1.0)
    cache_k_rope = normal((DEPTH, n_pool, PAGE_SIZE, QK_ROPE), 1.0)
    state_s5_re = normal((DEPTH, DEC_BATCH, S5_GROUPS, S5_STATE), 0.5)
    state_s5_im = normal((DEPTH, DEC_BATCH, S5_GROUPS, S5_STATE), 0.5)
    state_pool = normal((DEPTH, DEC_BATCH, POOL_BUF, W_BRANCH), 1.0)
    page_table = jax.random.permutation(next(ks), n_pool)[:n_used].reshape(DEC_BATCH, n_pages).astype(jnp.int32)

    norm_attn = 1.0 + normal((DEPTH, D_MODEL), 0.01)
    w_in = normal((DEPTH, D_MODEL, D_IN), D_MODEL ** -0.5)
    mla_q_norm = 1.0 + normal((DEPTH, Q_LORA), 0.01)
    mla_kv_norm = 1.0 + normal((DEPTH, KV_LORA), 0.01)
    mla_w_uq = normal((DEPTH, Q_LORA, N_HEADS, QK_NOPE + QK_ROPE), Q_LORA ** -0.5)
    mla_w_uk = normal((DEPTH, KV_LORA, N_HEADS, QK_NOPE), QK_NOPE ** -0.5)
    mla_w_uv = normal((DEPTH, KV_LORA, N_HEADS, V_HEAD), KV_LORA ** -0.5)
    s5_lambda_re = -0.5 + normal((DEPTH, S5_GROUPS, S5_STATE), 0.01)
    s5_lambda_im = (math.pi * jnp.arange(S5_STATE, dtype=jnp.float32))[None, None, :] + normal((DEPTH, S5_GROUPS, S5_STATE), 0.01)
    s5_log_dt = jax.random.uniform(next(ks), (DEPTH, S5_GROUPS), jnp.float32, math.log(DT_MIN), math.log(DT_MAX))
    s5_b_re = normal((DEPTH, S5_GROUPS, S5_STATE, S5_GC), (2 * S5_GC) ** -0.5)
    s5_b_im = normal((DEPTH, S5_GROUPS, S5_STATE, S5_GC), (2 * S5_GC) ** -0.5)
    s5_c_re = normal((DEPTH, S5_GROUPS, S5_GC, S5_STATE), S5_STATE ** -0.5)
    s5_c_im = normal((DEPTH, S5_GROUPS, S5_GC, S5_STATE), S5_STATE ** -0.5)
    s5_d = normal((DEPTH, S5_GROUPS, S5_GC), 1.0)
    s5_w_glu = normal((DEPTH, W_BRANCH, W_BRANCH), W_BRANCH ** -0.5)
    s5_b_glu = normal((DEPTH, W_BRANCH), 0.01)
    pool_w = normal((DEPTH, POOL_GROUPS, POOL_GW, POOL_GW), POOL_GW ** -0.5)
    pool_scale = 1.0 + normal((DEPTH, W_BRANCH), 0.01)
    gmlp_ln_g = 1.0 + normal((DEPTH, W_BRANCH), 0.01)
    gmlp_ln_b = normal((DEPTH, W_BRANCH), 0.01)
    gmlp_w_s = normal((DEPTH, GMLP_GROUPS, CHUNK, CHUNK), CHUNK ** -0.5)
    gmlp_b_s = 1.0 + normal((DEPTH, GMLP_GROUPS, CHUNK), 0.01)
    w_branch = normal((DEPTH, N_BRANCH, W_BRANCH, D_MODEL), W_BRANCH ** -0.5)
    w_out = normal((DEPTH, D_MODEL, D_MODEL), D_MODEL ** -0.5)
    norm_final = 1.0 + normal((D_MODEL,), 0.01)
    return {'x_prompt': x_prompt, 'x_sample': x_sample,
            'cache_kv_latent': cache_kv_latent, 'cache_k_rope': cache_k_rope,
            'state_s5_re': state_s5_re, 'state_s5_im': state_s5_im, 'state_pool': state_pool,
            'page_table': page_table,
            'norm_attn': norm_attn, 'w_in': w_in,
            'mla_q_norm': mla_q_norm, 'mla_kv_norm': mla_kv_norm,
            'mla_w_uq': mla_w_uq, 'mla_w_uk': mla_w_uk, 'mla_w_uv': mla_w_uv,
            's5_lambda_re': s5_lambda_re, 's5_lambda_im': s5_lambda_im, 's5_log_dt': s5_log_dt,
            's5_b_re': s5_b_re, 's5_b_im': s5_b_im, 's5_c_re': s5_c_re, 's5_c_im': s5_c_im,
            's5_d': s5_d, 's5_w_glu': s5_w_glu, 's5_b_glu': s5_b_glu,
            'pool_w': pool_w, 'pool_scale': pool_scale,
            'gmlp_ln_g': gmlp_ln_g, 'gmlp_ln_b': gmlp_ln_b, 'gmlp_w_s': gmlp_w_s, 'gmlp_b_s': gmlp_b_s,
            'w_branch': w_branch, 'w_out': w_out, 'norm_final': norm_final}


def _stack(states, i):
    return jnp.stack([s[i] for s in states], axis=0)


def reference(x_prompt, x_sample, cache_kv_latent, cache_k_rope, state_s5_re, state_s5_im,
              state_pool, page_table, norm_attn, w_in, mla_q_norm, mla_kv_norm,
              mla_w_uq, mla_w_uk, mla_w_uv, s5_lambda_re, s5_lambda_im, s5_log_dt,
              s5_b_re, s5_b_im, s5_c_re, s5_c_im, s5_d, s5_w_glu, s5_b_glu,
              pool_w, pool_scale, gmlp_ln_g, gmlp_ln_b, gmlp_w_s, gmlp_b_s,
              w_branch, w_out, norm_final):
    dec_b = page_table.shape[0]
    n_past = page_table.shape[1] * PAGE_SIZE
    pos_p = jnp.arange(x_prompt.shape[1], dtype=jnp.int32)
    pos_s = n_past + jnp.arange(x_sample.shape[1], dtype=jnp.int32)
    hp, hs = x_prompt, x_sample
    new_p, new_s = [], []
    for l in range(DEPTH):
        p = {'norm': norm_attn[l], 'w_in': w_in[l],
             'q_norm': mla_q_norm[l], 'kv_norm': mla_kv_norm[l],
             'w_uq': mla_w_uq[l], 'w_uk': mla_w_uk[l], 'w_uv': mla_w_uv[l],
             'lam_re': s5_lambda_re[l], 'lam_im': s5_lambda_im[l], 'log_dt': s5_log_dt[l],
             'b_re': s5_b_re[l], 'b_im': s5_b_im[l], 'c_re': s5_c_re[l], 'c_im': s5_c_im[l],
             'd': s5_d[l], 'w_glu': s5_w_glu[l], 'b_glu': s5_b_glu[l],
             'pool_w': pool_w[l], 'pool_scale': pool_scale[l],
             'ln_g': gmlp_ln_g[l], 'ln_b': gmlp_ln_b[l], 'w_s': gmlp_w_s[l], 'b_s': gmlp_b_s[l],
             'w_branch': w_branch[l], 'w_out': w_out[l]}
        hp, st_p = hybrid_layer(hp, pos_p, p, None)
        past = {'kv_lat': cache_kv_latent[l, page_table].reshape(dec_b, n_past, KV_LORA),
                'k_pe': cache_k_rope[l, page_table].reshape(dec_b, n_past, QK_ROPE),
                's5_re': state_s5_re[l], 's5_im': state_s5_im[l], 'pool': state_pool[l]}
        hs, st_s = hybrid_layer(hs, pos_s, p, past)
        new_p.append(st_p)
        new_s.append(st_s)
    y_prompt = rmsnorm(hp, norm_final)
    y_sample = rmsnorm(hs, norm_final)
    return (y_prompt, y_sample,
            _stack(new_p, 0), _stack(new_p, 1), _stack(new_p, 2), _stack(new_p, 3), _stack(new_p, 4),
            _stack(new_s, 0), _stack(new_s, 1), _stack(new_s, 2), _stack(new_s, 3), _stack(new_s, 4),
            _stack(new_s, 5))
```

```python
import functools
import math

import jax
import jax.numpy as jnp
from jax import lax
from jax.experimental import pallas as pl
from jax.experimental.pallas import tpu as pltpu

F32 = jnp.float32
BF16 = jnp.bfloat16

NORM_EPS = 1e-6
ROPE_THETA = 10000.0
POOL_WINDOWS = (2, 4, 8, 16)
S5_CHUNK = 16
LANES = 128
VMEM_LIMIT = 56 * 1024 * 1024
GELU_C = math.sqrt(2.0 / math.pi)
ROW_TILE = 512
INPROJ_TILE = 1024
ATTN_TILE = 1024
DECODE_PAGES = 32


def _cparams(*sem):
    return pltpu.CompilerParams(dimension_semantics=sem, vmem_limit_bytes=VMEM_LIMIT)


def _rms(x, g):
    return x * lax.rsqrt(jnp.mean(x * x, axis=-1, keepdims=True) + NORM_EPS) * g


def _sigmoid(x):
    return 1.0 / (1.0 + jnp.exp(-x))


def _silu(x):
    return x * _sigmoid(x)


def _gelu_tanh(x):
    return x * (0.5 * (1.0 + jnp.tanh(GELU_C * (x + 0.044715 * (x * x * x)))))


def _rope128(x, c, s1, s2):
    return x * c + pltpu.roll(x, 96, axis=1) * s1 + pltpu.roll(x, 32, axis=1) * s2


def _dot(a, b):
    return jnp.dot(a, b, preferred_element_type=F32)


def _dot_nt(a, b):
    return lax.dot_general(a, b, (((1,), (1,)), ((), ())), preferred_element_type=F32)


def _norm_inproj_kernel(x_ref, g_ref, w_ref, proj_ref, xn_ref):
    @pl.when(pl.program_id(1) == 0)
    def _():
        xn_ref[...] = _rms(x_ref[...], g_ref[...]).astype(BF16)

    proj_ref[...] = _dot(xn_ref[...], w_ref[...])


def norm_inproj(x, g, w, tm, tn):
    m, d = x.shape
    n = w.shape[1]
    return pl.pallas_call(
        _norm_inproj_kernel,
        grid=(m // tm, n // tn),
        in_specs=[pl.BlockSpec((tm, d), lambda i, j: (i, 0)),
                  pl.BlockSpec((1, d), lambda i, j: (0, 0)),
                  pl.BlockSpec((d, tn), lambda i, j: (0, j))],
        out_specs=[pl.BlockSpec((tm, tn), lambda i, j: (i, j)),
                   pl.BlockSpec((tm, d), lambda i, j: (i, 0))],
        out_shape=[jax.ShapeDtypeStruct((m, n), F32), jax.ShapeDtypeStruct((m, d), BF16)],
        compiler_params=_cparams("parallel", "arbitrary"),
        name="norm_inproj",
    )(x, g, w)


def _merge_kernel(b0_ref, b1_ref, b2_ref, b3_ref, g0_ref, g1_ref, g2_ref, g3_ref, w_ref, o_ref):
    acc = None
    for k, (b_ref, g_ref) in enumerate(((b0_ref, g0_ref), (b1_ref, g1_ref),
                                        (b2_ref, g2_ref), (b3_ref, g3_ref))):
        t = _sigmoid(g_ref[...]) * _dot(b_ref[...], w_ref[k])
        acc = t if acc is None else acc + t
    o_ref[...] = acc.astype(BF16)


def merge_branches(branches, proj, gate_off, w_branch, tm, tn):
    m, w = branches[0].shape
    d = w_branch.shape[2]
    nb = len(branches)
    gate_specs = [pl.BlockSpec((tm, tn), functools.partial(
        lambda i, j, k: (i, (gate_off + k * d) // tn + j), k=k)) for k in range(nb)]
    return pl.pallas_call(
        _merge_kernel,
        grid=(m // tm, d // tn),
        in_specs=[pl.BlockSpec((tm, w), lambda i, j: (i, 0))] * nb + gate_specs
        + [pl.BlockSpec((nb, w, tn), lambda i, j: (0, 0, j))],
        out_specs=pl.BlockSpec((tm, tn), lambda i, j: (i, j)),
        out_shape=jax.ShapeDtypeStruct((m, d), BF16),
        compiler_params=_cparams("parallel", "arbitrary"),
        name="merge_branches",
    )(*branches, *([proj] * nb), w_branch)


def _outproj_kernel(h_ref, x_ref, w_ref, o_ref):
    o_ref[...] = h_ref[...] + _dot(x_ref[...], w_ref[...])


def _outproj_norm_kernel(h_ref, x_ref, w_ref, g_ref, o_ref):
    o_ref[...] = _rms(h_ref[...] + _dot(x_ref[...], w_ref[...]), g_ref[...])


def out_proj(h, merged, w_out, tm, final_gain=None):
    m, d = h.shape
    specs = [pl.BlockSpec((tm, d), lambda i: (i, 0)),
             pl.BlockSpec((tm, d), lambda i: (i, 0)),
             pl.BlockSpec((d, d), lambda i: (0, 0))]
    args = [h, merged, w_out]
    body = _outproj_kernel
    if final_gain is not None:
        specs.append(pl.BlockSpec((1, d), lambda i: (0, 0)))
        args.append(final_gain)
        body = _outproj_norm_kernel
    return pl.pallas_call(
        body,
        grid=(m // tm,),
        in_specs=specs,
        out_specs=pl.BlockSpec((tm, d), lambda i: (i, 0)),
        out_shape=jax.ShapeDtypeStruct((m, d), F32),
        compiler_params=_cparams("parallel"),
        name="out_proj",
    )(*args)


def _kv_side_kernel(ckv_ref, xn_ref, wkpe_ref, g_ref, wuk_ref, wuv_ref, c_ref, s1_ref, s2_ref,
                    klat_ref, kpe_ref, kp_ref, v_ref, *, n_heads, d_nope, d_v, d_rope):
    klat = _rms(ckv_ref[...], g_ref[...])
    klat_ref[...] = klat
    kl16 = klat.astype(BF16)
    kpe = _rope128(_dot(xn_ref[...], wkpe_ref[...]), c_ref[...], s1_ref[...], s2_ref[...])
    kpe_ref[...] = kpe[:, :d_rope]
    kpe16 = kpe.astype(BF16)
    knope = _dot(kl16, wuk_ref[...])
    v = _dot(kl16, wuv_ref[...])
    for h in range(n_heads):
        kp_ref[0, h, :, 0:d_nope] = knope[:, h * d_nope:(h + 1) * d_nope].astype(BF16)
        kp_ref[0, h, :, d_nope:d_nope + LANES] = kpe16
        v_ref[0, h] = v[:, h * d_v:(h + 1) * d_v].astype(BF16)


def _q_side_kernel(cq_ref, g_ref, wuq_ref, c_ref, s1_ref, s2_ref, qp_ref, *, n_heads, d_nope, scale):
    cq = _rms(cq_ref[...], g_ref[...]).astype(BF16)
    q = _dot(cq, wuq_ref[...])
    c, s1, s2 = c_ref[...], s1_ref[...], s2_ref[...]
    dq = d_nope + LANES
    for h in range(n_heads):
        qp_ref[0, h, :, 0:d_nope] = (q[:, h * dq:h * dq + d_nope] * scale).astype(BF16)
        pe = _rope128(q[:, h * dq + d_nope:(h + 1) * dq], c, s1, s2)
        qp_ref[0, h, :, d_nope:dq] = (pe * scale).astype(BF16)


def _flash_kernel(q_ref, k_ref, v_ref, g_ref, o_ref, m_sc, l_sc, acc_sc):
    i = pl.program_id(2)
    j = pl.program_id(3)

    @pl.when(j == 0)
    def _():
        m_sc[...] = jnp.full(m_sc.shape, -jnp.inf, F32)
        l_sc[...] = jnp.zeros(l_sc.shape, F32)
        acc_sc[...] = jnp.zeros(acc_sc.shape, F32)

    def update(masked):
        s = _dot_nt(q_ref[0, 0], k_ref[0, 0])
        if masked:
            row = lax.broadcasted_iota(jnp.int32, s.shape, 0)
            col = lax.broadcasted_iota(jnp.int32, s.shape, 1)
            s = jnp.where(col <= row, s, -jnp.inf)
        m_old = m_sc[...]
        m_new = jnp.maximum(m_old, jnp.max(s, axis=-1, keepdims=True))
        alpha = jnp.exp(m_old - m_new)
        p = jnp.exp(s - m_new)
        l_sc[...] = alpha * l_sc[...] + jnp.sum(p, axis=-1, keepdims=True)
        acc_sc[...] = alpha * acc_sc[...] + _dot(p.astype(BF16), v_ref[0, 0])
        m_sc[...] = m_new

    @pl.when(j < i)
    def _():
        update(False)

    @pl.when(j == i)
    def _():
        update(True)
        o_ref[...] = (acc_sc[...] / l_sc[...] * _silu(g_ref[...])).astype(BF16)


def prompt_mla(proj, xn, lp, dims, rope_tabs):
    b, l, h = dims["B"], dims["L"], dims["H"]
    d_nope, d_v, d_rope = dims["QK_NOPE"], dims["V_HEAD"], dims["QK_ROPE"]
    r_q, r_kv, d = dims["Q_LORA"], dims["KV_LORA"], dims["D"]
    off = dims["off"]
    m = b * l
    tm = min(ROW_TILE, l)
    nl = l // tm
    dk = d_nope + LANES
    c_tab, s1_tab, s2_tab = rope_tabs
    tab_spec = pl.BlockSpec((tm, LANES), lambda i: (i % nl, 0))

    klat, kpe, kp, v = pl.pallas_call(
        functools.partial(_kv_side_kernel, n_heads=h, d_nope=d_nope, d_v=d_v, d_rope=d_rope),
        grid=(m // tm,),
        in_specs=[pl.BlockSpec((tm, r_kv), lambda i: (i, off["c_kv"] // r_kv)),
                  pl.BlockSpec((tm, d), lambda i: (i, 0)),
                  pl.BlockSpec((d, LANES), lambda i: (0, 0)),
                  pl.BlockSpec((1, r_kv), lambda i: (0, 0)),
                  pl.BlockSpec((r_kv, h * d_nope), lambda i: (0, 0)),
                  pl.BlockSpec((r_kv, h * d_v), lambda i: (0, 0)),
                  tab_spec, tab_spec, tab_spec],
        out_specs=[pl.BlockSpec((tm, r_kv), lambda i: (i, 0)),
                   pl.BlockSpec((tm, d_rope), lambda i: (i, 0)),
                   pl.BlockSpec((1, h, tm, dk), lambda i: (i // nl, 0, i % nl, 0)),
                   pl.BlockSpec((1, h, tm, d_v), lambda i: (i // nl, 0, i % nl, 0))],
        out_shape=[jax.ShapeDtypeStruct((m, r_kv), F32),
                   jax.ShapeDtypeStruct((m, d_rope), F32),
                   jax.ShapeDtypeStruct((b, h, l, dk), BF16),
                   jax.ShapeDtypeStruct((b, h, l, d_v), BF16)],
        compiler_params=_cparams("parallel"),
        name="prompt_kv_side",
    )(proj, xn, lp["w_kpe"], lp["kv_norm"], lp["w_uk_flat"], lp["w_uv_flat"], c_tab, s1_tab, s2_tab)

    qp = pl.pallas_call(
        functools.partial(_q_side_kernel, n_heads=h, d_nope=d_nope, scale=dims["scale"]),
        grid=(m // tm,),
        in_specs=[pl.BlockSpec((tm, r_q), lambda i: (i, off["c_q"] // r_q)),
                  pl.BlockSpec((1, r_q), lambda i: (0, 0)),
                  pl.BlockSpec((r_q, h * dk), lambda i: (0, 0)),
                  tab_spec, tab_spec, tab_spec],
        out_specs=pl.BlockSpec((1, h, tm, dk), lambda i: (i // nl, 0, i % nl, 0)),
        out_shape=jax.ShapeDtypeStruct((b, h, l, dk), BF16),
        compiler_params=_cparams("parallel"),
        name="prompt_q_side",
    )(proj, lp["q_norm"], lp["w_uq_pad"], c_tab, s1_tab, s2_tab)

    tq = min(ATTN_TILE, l)
    nq = l // tq
    br = pl.pallas_call(
        _flash_kernel,
        grid=(b, h, nq, nq),
        in_specs=[pl.BlockSpec((1, 1, tq, dk), lambda bi, hi, i, j: (bi, hi, i, 0)),
                  pl.BlockSpec((1, 1, tq, dk), lambda bi, hi, i, j: (bi, hi, jnp.minimum(i, j), 0)),
                  pl.BlockSpec((1, 1, tq, d_v), lambda bi, hi, i, j: (bi, hi, jnp.minimum(i, j), 0)),
                  pl.BlockSpec((tq, d_v), lambda bi, hi, i, j: (bi * nq + i, off["g_mla"] // d_v + hi))],
        out_specs=pl.BlockSpec((tq, d_v), lambda bi, hi, i, j: (bi * nq + i, hi)),
        out_shape=jax.ShapeDtypeStruct((m, h * d_v), BF16),
        scratch_shapes=[pltpu.VMEM((tq, 1), F32), pltpu.VMEM((tq, 1), F32), pltpu.VMEM((tq, d_v), F32)],
        compiler_params=_cparams("parallel", "parallel", "parallel", "arbitrary"),
        name="prompt_flash",
    )(qp, kp, v, proj)
    return br, klat, kpe


def _s5_chunk_kernel(u_ref, kmat_ref, smat_ref, omat_ref, ar_ref, ai_ref, y_ref, hfin_ref, *, n_chunks, n_batch):
    u16 = u_ref[0].astype(BF16)
    y_intra = _dot(u16, kmat_ref[0])
    x = _dot(u16, smat_ref[0])
    rows = x.shape[0]
    half = x.shape[1] // 2
    nidx = lax.rem(lax.broadcasted_iota(jnp.int32, (rows, 1), 0), n_chunks)
    ar, ai = ar_ref[0], ai_ref[0]
    d = 1
    while d < n_chunks:
        xs = jnp.where(nidx >= d, pltpu.roll(x, d, axis=0), 0.0)
        x = x + xs * ar + pltpu.roll(xs, half, axis=1) * ai
        ar, ai = ar * ar - ai * ai, 2.0 * ar * ai
        d *= 2
    h_prev = jnp.where(nidx >= 1, pltpu.roll(x, 1, axis=0), 0.0)
    y_ref[0] = y_intra + _dot(h_prev.astype(BF16), omat_ref[0])
    for bi in range(n_batch):
        r = bi * n_chunks + n_chunks - 1
        hfin_ref[0, bi:bi + 1, :] = x[r:r + 1, :]


def _glu_gate_kernel(y_ref, w_ref, b_ref, g_ref, o_ref):
    z = _gelu_tanh(y_ref[...])
    o = z * _sigmoid(_dot(z.astype(BF16), w_ref[...]) + b_ref[...])
    o_ref[...] = (o * _silu(g_ref[...])).astype(BF16)


def prompt_s5(proj, lp, dims):
    b, l, w, off = dims["B"], dims["L"], dims["W"], dims["off"]
    g, gc, p = dims["S5_G"], dims["S5_GC"], dims["S5_P"]
    t = S5_CHUNK
    nc = l // t
    m = b * l
    u = proj[:, off["u_s5"]:off["u_s5"] + w]
    u = u.reshape(b, nc, t, g, gc).transpose(3, 0, 1, 2, 4).reshape(g, b * nc, t * gc)
    y, hfin = pl.pallas_call(
        functools.partial(_s5_chunk_kernel, n_chunks=nc, n_batch=b),
        grid=(g,),
        in_specs=[pl.BlockSpec((1, b * nc, t * gc), lambda i: (i, 0, 0)),
                  pl.BlockSpec((1, t * gc, t * gc), lambda i: (i, 0, 0)),
                  pl.BlockSpec((1, t * gc, 2 * p), lambda i: (i, 0, 0)),
                  pl.BlockSpec((1, 2 * p, t * gc), lambda i: (i, 0, 0)),
                  pl.BlockSpec((1, 1, 2 * p), lambda i: (i, 0, 0)),
                  pl.BlockSpec((1, 1, 2 * p), lambda i: (i, 0, 0))],
        out_specs=[pl.BlockSpec((1, b * nc, t * gc), lambda i: (i, 0, 0)),
                   pl.BlockSpec((1, b, 2 * p), lambda i: (i, 0, 0))],
        out_shape=[jax.ShapeDtypeStruct((g, b * nc, t * gc), F32),
                   jax.ShapeDtypeStruct((g, b, 2 * p), F32)],
        compiler_params=_cparams("parallel"),
        name="prompt_s5_chunks",
    )(u, lp["s5_kmat"], lp["s5_smat"], lp["s5_omat"], lp["s5_at_r"], lp["s5_at_i"])
    y = y.reshape(g, b, nc, t, gc).transpose(1, 2, 3, 0, 4).reshape(m, w)
    tm = min(ROW_TILE, l)
    br = pl.pallas_call(
        _glu_gate_kernel,
        grid=(m // tm,),
        in_specs=[pl.BlockSpec((tm, w), lambda i: (i, 0)),
                  pl.BlockSpec((w, w), lambda i: (0, 0)),
                  pl.BlockSpec((1, w), lambda i: (0, 0)),
                  pl.BlockSpec((tm, w), lambda i: (i, off["g_s5"] // w))],
        out_specs=pl.BlockSpec((tm, w), lambda i: (i, 0)),
        out_shape=jax.ShapeDtypeStruct((m, w), BF16),
        compiler_params=_cparams("parallel"),
        name="prompt_glu_gate",
    )(y, lp["w_glu"], lp["b_glu"], proj)
    hfin = hfin.transpose(1, 0, 2)
    return br, hfin[:, :, :p], hfin[:, :, p:]


def _pool_kernel(u_ref, halo_ref, g_ref, w_ref, sc_ref, o_ref, *, tiles_per_seq, halo):
    i = pl.program_id(0)
    x = u_ref[...]
    tm, width = x.shape
    gw = width // len(POOL_WINDOWS)
    first = (i % tiles_per_seq) == 0
    hal = jnp.where(first, 0.0, halo_ref[...])
    ext = jnp.concatenate([hal, x], axis=0)
    pos = lax.broadcasted_iota(jnp.int32, (tm, 1), 0) + (i % tiles_per_seq) * tm
    outs = []
    for gi, win in enumerate(POOL_WINDOWS):
        xg = x[:, gi * gw:(gi + 1) * gw]
        eg = ext[:, gi * gw:(gi + 1) * gw]
        s = xg
        for k in range(1, win):
            s = s + pltpu.roll(eg, k, axis=0)[halo:halo + tm]
        cnt = jnp.minimum(pos + 1, win).astype(F32)
        dlt = s / cnt - xg
        outs.append(_dot(dlt.astype(BF16), w_ref[gi]))
    y = jnp.concatenate(outs, axis=1) * sc_ref[...]
    o_ref[...] = (y * _silu(g_ref[...])).astype(BF16)


def _gmlp_kernel(u_ref, v_ref, g_ref, lng_ref, lnb_ref, ws_ref, bs_ref, o_ref, *, chunk):
    v = v_ref[...]
    tm, width = v.shape
    n_groups = ws_ref.shape[0]
    gw = width // n_groups
    mu = jnp.mean(v, axis=-1, keepdims=True)
    vc = v - mu
    vn = vc * lax.rsqrt(jnp.mean(vc * vc, axis=-1, keepdims=True) + NORM_EPS) * lng_ref[...] + lnb_ref[...]
    vn16 = vn.astype(BF16)
    bs = bs_ref[...]
    rows = []
    for c in range(tm // chunk):
        cols = []
        for gi in range(n_groups):
            blk = vn16[c * chunk:(c + 1) * chunk, gi * gw:(gi + 1) * gw]
            cols.append(_dot(ws_ref[gi], blk) + bs[:, gi:gi + 1])
        rows.append(jnp.concatenate(cols, axis=1))
    mixed = jnp.concatenate(rows, axis=0)
    o_ref[...] = (u_ref[...] * mixed * _silu(g_ref[...])).astype(BF16)


def prompt_pool(proj, lp, dims):
    b, l, w, off = dims["B"], dims["L"], dims["W"], dims["off"]
    m = b * l
    tm = min(ROW_TILE, l)
    halo = 16
    nl = l // tm
    n_g = len(POOL_WINDOWS)
    return pl.pallas_call(
        functools.partial(_pool_kernel, tiles_per_seq=nl, halo=halo),
        grid=(m // tm,),
        in_specs=[pl.BlockSpec((tm, w), lambda i: (i, off["u_pool"] // w)),
                  pl.BlockSpec((halo, w), lambda i: (jnp.maximum(i * (tm // halo) - 1, 0), off["u_pool"] // w)),
                  pl.BlockSpec((tm, w), lambda i: (i, off["g_pool"] // w)),
                  pl.BlockSpec((n_g, w // n_g, w // n_g), lambda i: (0, 0, 0)),
                  pl.BlockSpec((1, w), lambda i: (0, 0))],
        out_specs=pl.BlockSpec((tm, w), lambda i: (i, 0)),
        out_shape=jax.ShapeDtypeStruct((m, w), BF16),
        compiler_params=_cparams("parallel"),
        name="prompt_pool",
    )(proj, proj, proj, lp["pool_w"], lp["pool_scale"])


def prompt_gmlp(proj, lp, dims):
    b, l, w, off = dims["B"], dims["L"], dims["W"], dims["off"]
    m = b * l
    chunk = dims["CHUNK"]
    tm = min(ROW_TILE, l)
    n_g = lp["gmlp_ws"].shape[0]
    return pl.pallas_call(
        functools.partial(_gmlp_kernel, chunk=chunk),
        grid=(m // tm,),
        in_specs=[pl.BlockSpec((tm, w), lambda i: (i, off["u_gm"] // w)),
                  pl.BlockSpec((tm, w), lambda i: (i, off["v_gm"] // w)),
                  pl.BlockSpec((tm, w), lambda i: (i, off["g_gm"] // w)),
                  pl.BlockSpec((1, w), lambda i: (0, 0)),
                  pl.BlockSpec((1, w), lambda i: (0, 0)),
                  pl.BlockSpec((n_g, chunk, chunk), lambda i: (0, 0, 0)),
                  pl.BlockSpec((chunk, n_g), lambda i: (0, 0))],
        out_specs=pl.BlockSpec((tm, w), lambda i: (i, 0)),
        out_shape=jax.ShapeDtypeStruct((m, w), BF16),
        compiler_params=_cparams("parallel"),
        name="prompt_gmlp",
    )(proj, proj, proj, lp["ln_g"], lp["ln_b"], lp["gmlp_ws"], lp["gmlp_bs_t"])


def _sample_qkv_kernel(p_ref, xn_ref, wkpe_ref, qg_ref, kg_ref, wuq_ref, wukt_ref, c_ref, s1_ref, s2_ref,
                       klat_ref, kpe_ref, qlat_ref, qpe_ref, *, n_heads, d_nope, r_q, r_kv, d_rope, scale):
    pr = p_ref[...]
    c, s1, s2 = c_ref[...], s1_ref[...], s2_ref[...]
    klat_ref[...] = _rms(pr[:, r_q:r_q + r_kv], kg_ref[...])
    kpe = _rope128(_dot(xn_ref[...], wkpe_ref[...]), c, s1, s2)
    kpe_ref[...] = kpe[:, :d_rope]
    cq = _rms(pr[:, :r_q], qg_ref[...]).astype(BF16)
    q = _dot(cq, wuq_ref[...])
    dq = d_nope + LANES
    for h in range(n_heads):
        qn = q[:, h * dq:h * dq + d_nope].astype(BF16)
        qlat_ref[h] = (_dot(qn, wukt_ref[h]) * scale).astype(BF16)
        qpe_ref[h] = (_rope128(q[:, h * dq + d_nope:(h + 1) * dq], c, s1, s2) * scale).astype(BF16)


def sample_qkv(proj, xn, lp, dims, rope_row):
    ms, h = dims["MS"], dims["H"]
    d_nope, d_rope, r_q, r_kv, d = dims["QK_NOPE"], dims["QK_ROPE"], dims["Q_LORA"], dims["KV_LORA"], dims["D"]
    dk = d_nope + LANES
    full = lambda *shape: pl.BlockSpec(shape, lambda i: (0,) * len(shape))
    return pl.pallas_call(
        functools.partial(_sample_qkv_kernel, n_heads=h, d_nope=d_nope, r_q=r_q, r_kv=r_kv,
                          d_rope=d_rope, scale=dims["scale"]),
        grid=(1,),
        in_specs=[pl.BlockSpec((ms, r_q + r_kv), lambda i: (0, 0)),
                  full(ms, d), full(d, LANES), full(1, r_q), full(1, r_kv), full(r_q, h * dk),
                  full(h, d_nope, r_kv), full(1, LANES), full(1, LANES), full(1, LANES)],
        out_specs=[full(ms, r_kv), full(ms, d_rope), full(h, ms, r_kv), full(h, ms, LANES)],
        out_shape=[jax.ShapeDtypeStruct((ms, r_kv), F32), jax.ShapeDtypeStruct((ms, d_rope), F32),
                   jax.ShapeDtypeStruct((h, ms, r_kv), BF16), jax.ShapeDtypeStruct((h, ms, LANES), BF16)],
        compiler_params=_cparams("arbitrary"),
        name="sample_qkv",
    )(proj, xn, lp["w_kpe"], lp["q_norm"], lp["kv_norm"], lp["w_uq_pad"], lp["w_uk_t"], *rope_row)


def _decode_kernel(pt_ref, qlat_ref, qpe_ref, knl_ref, knp_ref, ckv_hbm, ckr_hbm, o_ref,
                   kvbuf, krbuf, sems, m_sc, l_sc, acc_sc, *, layer, pages_per_chunk, n_chunks, page, d_rope):
    b = pl.program_id(0)
    c = pl.program_id(1)
    step = b * n_chunks + c
    n_steps = pl.num_programs(0) * n_chunks
    slot = step % 2

    def copies(bb, cc, sl):
        out = []
        for pi in range(pages_per_chunk):
            pid = pt_ref[bb, cc * pages_per_chunk + pi]
            out.append(pltpu.make_async_copy(ckv_hbm.at[layer, pid],
                                             kvbuf.at[sl, pl.ds(pi * page, page)], sems.at[0, sl]))
            out.append(pltpu.make_async_copy(ckr_hbm.at[layer, pid],
                                             krbuf.at[sl, pl.ds(pi * page, page)], sems.at[1, sl]))
        return out

    @pl.when(step == 0)
    def _():
        for cp in copies(b, c, slot):
            cp.start()

    @pl.when(step + 1 < n_steps)
    def _():
        nxt = step + 1
        for cp in copies(nxt // n_chunks, nxt % n_chunks, 1 - slot):
            cp.start()

    for cp in copies(b, c, slot):
        cp.wait()

    @pl.when(c == 0)
    def _():
        m_sc[...] = jnp.full(m_sc.shape, -jnp.inf, F32)
        l_sc[...] = jnp.zeros(l_sc.shape, F32)
        acc_sc[...] = jnp.zeros(acc_sc.shape, F32)

    qlat = qlat_ref[0]
    qpe = qpe_ref[0][:, :d_rope]
    kv16 = kvbuf[slot].astype(BF16)
    kr16 = krbuf[slot].astype(BF16)
    s = _dot_nt(qlat, kv16) + _dot_nt(qpe, kr16)
    m_old = m_sc[...]
    m_new = jnp.maximum(m_old, jnp.max(s, axis=-1, keepdims=True))
    alpha = jnp.exp(m_old - m_new)
    p = jnp.exp(s - m_new)
    l_new = alpha * l_sc[...] + jnp.sum(p, axis=-1, keepdims=True)
    acc_new = alpha * acc_sc[...] + _dot(p.astype(BF16), kv16)
    m_sc[...] = m_new
    l_sc[...] = l_new
    acc_sc[...] = acc_new

    @pl.when(c == n_chunks - 1)
    def _():
        knl = knl_ref[0]
        s_new = (jnp.sum(qlat.astype(F32) * knl, axis=-1, keepdims=True)
                 + jnp.sum(qpe.astype(F32) * knp_ref[0], axis=-1, keepdims=True))
        m_fin = jnp.maximum(m_new, s_new)
        a_old = jnp.exp(m_new - m_fin)
        p_new = jnp.exp(s_new - m_fin)
        o_ref[0] = (a_old * acc_new + p_new * knl) / (a_old * l_new + p_new)


def sample_decode(page_table, qlat, qpe, k_new_lat, k_new_pe, cache_kv, cache_kr, layer, dims):
    ms, h, r_kv, d_rope = dims["MS"], dims["H"], dims["KV_LORA"], dims["QK_ROPE"]
    n_pages = page_table.shape[1]
    page = cache_kv.shape[2]
    ppc = min(DECODE_PAGES, n_pages)
    n_chunks = n_pages // ppc
    grid_spec = pltpu.PrefetchScalarGridSpec(
        num_scalar_prefetch=1,
        grid=(ms, n_chunks),
        in_specs=[pl.BlockSpec((1, h, r_kv), lambda b, c, pt: (b, 0, 0)),
                  pl.BlockSpec((1, h, LANES), lambda b, c, pt: (b, 0, 0)),
                  pl.BlockSpec((1, 1, r_kv), lambda b, c, pt: (b, 0, 0)),
                  pl.BlockSpec((1, 1, d_rope), lambda b, c, pt: (b, 0, 0)),
                  pl.BlockSpec(memory_space=pl.ANY),
                  pl.BlockSpec(memory_space=pl.ANY)],
        out_specs=pl.BlockSpec((1, h, r_kv), lambda b, c, pt: (b, 0, 0)),
        scratch_shapes=[pltpu.VMEM((2, ppc * page, r_kv), F32),
                        pltpu.VMEM((2, ppc * page, d_rope), F32),
                        pltpu.SemaphoreType.DMA((2, 2)),
                        pltpu.VMEM((h, 1), F32), pltpu.VMEM((h, 1), F32), pltpu.VMEM((h, r_kv), F32)])
    return pl.pallas_call(
        functools.partial(_decode_kernel, layer=layer, pages_per_chunk=ppc, n_chunks=n_chunks,
                          page=page, d_rope=d_rope),
        grid_spec=grid_spec,
        out_shape=jax.ShapeDtypeStruct((ms, h, r_kv), F32),
        compiler_params=_cparams("arbitrary", "arbitrary"),
        name="sample_decode",
    )(page_table, qlat, qpe, k_new_lat.reshape(ms, 1, r_kv), k_new_pe.reshape(ms, 1, d_rope), cache_kv, cache_kr)


def _sample_s5_kernel(u_ref, g_ref, h0r_ref, h0i_ref, ar_ref, ai_ref, bs_ref, cs_ref, d_ref, wglu_ref, bglu_ref,
                      br_ref, hr_ref, hi_ref):
    u = u_ref[...]
    u16 = u.astype(BF16)
    n_kt, kt, two_ns = bs_ref.shape
    ns = two_ns // 2
    ys = []
    for t in range(n_kt):
        bu = _dot(u16[:, t * kt:(t + 1) * kt], bs_ref[t])
        sl = slice(t * ns, (t + 1) * ns)
        ar, ai, h0r, h0i = ar_ref[:, sl], ai_ref[:, sl], h0r_ref[:, sl], h0i_ref[:, sl]
        hr = ar * h0r - ai * h0i + bu[:, :ns]
        hi = ar * h0i + ai * h0r + bu[:, ns:]
        hr_ref[:, sl] = hr
        hi_ref[:, sl] = hi
        hcat = jnp.concatenate([hr, hi], axis=1).astype(BF16)
        ys.append(_dot(hcat, cs_ref[t]))
    y = jnp.concatenate(ys, axis=1) + d_ref[...] * u
    z = _gelu_tanh(y)
    o = z * _sigmoid(_dot(z.astype(BF16), wglu_ref[...]) + bglu_ref[...])
    br_ref[...] = (o * _silu(g_ref[...])).astype(BF16)


def sample_s5(proj, h0r, h0i, lp, dims):
    ms, w, off = dims["MS"], dims["W"], dims["off"]
    n_state = h0r.shape[1]
    tr = min(64, ms)
    n_kt, kt, two_ns = lp["s5_bs"].shape
    row = lambda width, cb=0: pl.BlockSpec((tr, width), lambda i: (i, cb))
    full = lambda *shape: pl.BlockSpec(shape, lambda i: (0,) * len(shape))
    return pl.pallas_call(
        _sample_s5_kernel,
        grid=(ms // tr,),
        in_specs=[row(w, off["u_s5"] // w), row(w, off["g_s5"] // w), row(n_state), row(n_state),
                  full(1, n_state), full(1, n_state), full(n_kt, kt, two_ns), full(n_kt, two_ns, kt),
                  full(1, w), full(w, w), full(1, w)],
        out_specs=[row(w), row(n_state), row(n_state)],
        out_shape=[jax.ShapeDtypeStruct((ms, w), BF16), jax.ShapeDtypeStruct((ms, n_state), F32),
                   jax.ShapeDtypeStruct((ms, n_state), F32)],
        compiler_params=_cparams("parallel"),
        name="sample_s5",
    )(proj, proj, h0r, h0i, lp["s5_a_re"], lp["s5_a_im"], lp["s5_bs"], lp["s5_cs"], lp["s5_d"],
      lp["w_glu"], lp["b_glu"])


def _sample_mix_kernel(olat_ref, wuv_ref, gmla_ref, up_ref, gp_ref, past_ref, pw_ref, psc_ref,
                       ug_ref, vg_ref, gg_ref, lng_ref, lnb_ref, ws0_ref, bs0_ref,
                       bmla_ref, bpool_ref, bgm_ref, vrow_ref, *, n_heads, d_v, pool_cnt):
    gm = gmla_ref[...]
    for h in range(n_heads):
        o = _dot(olat_ref[h].astype(BF16), wuv_ref[h])
        bmla_ref[:, h * d_v:(h + 1) * d_v] = (o * _silu(gm[:, h * d_v:(h + 1) * d_v])).astype(BF16)

    u = up_ref[...]
    n_past = past_ref.shape[0]
    gw = u.shape[1] // len(POOL_WINDOWS)
    outs = []
    for gi, win in enumerate(POOL_WINDOWS):
        sl = slice(gi * gw, (gi + 1) * gw)
        s = u[:, sl]
        for k in range(1, win):
            s = s + past_ref[n_past - k][:, sl]
        dlt = s / pool_cnt[gi] - u[:, sl]
        outs.append(_dot(dlt.astype(BF16), pw_ref[gi]))
    yp = jnp.concatenate(outs, axis=1) * psc_ref[...]
    bpool_ref[...] = (yp * _silu(gp_ref[...])).astype(BF16)

    v = vg_ref[...]
    mu = jnp.mean(v, axis=-1, keepdims=True)
    vc = v - mu
    vn = vc * lax.rsqrt(jnp.mean(vc * vc, axis=-1, keepdims=True) + NORM_EPS) * lng_ref[...] + lnb_ref[...]
    vrow_ref[...] = vn
    mixed = ws0_ref[...] * vn + bs0_ref[...]
    bgm_ref[...] = (ug_ref[...] * mixed * _silu(gg_ref[...])).astype(BF16)


def sample_mix(proj, olat_hm, past_t, lp, dims):
    ms, w, h, d_v, r_kv, off = dims["MS"], dims["W"], dims["H"], dims["V_HEAD"], dims["KV_LORA"], dims["off"]
    tr = min(64, ms)
    n_past = past_t.shape[0]
    n_g = len(POOL_WINDOWS)
    pool_cnt = tuple(float(min(dims["N_PAST"] + 1, win)) for win in POOL_WINDOWS)
    row = lambda key: pl.BlockSpec((tr, w), lambda i: (i, off[key] // w))
    full = lambda *shape: pl.BlockSpec(shape, lambda i: (0,) * len(shape))
    out_row = pl.BlockSpec((tr, w), lambda i: (i, 0))
    return pl.pallas_call(
        functools.partial(_sample_mix_kernel, n_heads=h, d_v=d_v, pool_cnt=pool_cnt),
        grid=(ms // tr,),
        in_specs=[pl.BlockSpec((h, tr, r_kv), lambda i: (0, i, 0)), full(h, r_kv, d_v), row("g_mla"),
                  row("u_pool"), row("g_pool"), pl.BlockSpec((n_past, tr, w), lambda i: (0, i, 0)),
                  full(n_g, w // n_g, w // n_g), full(1, w),
                  row("u_gm"), row("v_gm"), row("g_gm"), full(1, w), full(1, w), full(1, w), full(1, w)],
        out_specs=[out_row, out_row, out_row, out_row],
        out_shape=[jax.ShapeDtypeStruct((ms, w), BF16), jax.ShapeDtypeStruct((ms, w), BF16),
                   jax.ShapeDtypeStruct((ms, w), BF16), jax.ShapeDtypeStruct((ms, w), F32)],
        compiler_params=_cparams("parallel"),
        name="sample_mix",
    )(olat_hm, lp["w_uv_h"], proj, proj, proj, past_t, lp["pool_w"], lp["pool_scale"],
      proj, proj, proj, lp["ln_g"], lp["ln_b"], lp["gmlp_ws0"], lp["gmlp_bs0"])


def _s5_params(lam_re, lam_im, log_dt, b_re, b_im, c_re, c_im, d):
    hp = lax.Precision.HIGHEST
    g, p, gc = b_re.shape
    t = S5_CHUNK
    dt = jnp.exp(log_dt)[:, None]
    ld_re, ld_im = lam_re * dt, lam_im * dt
    mag = jnp.exp(ld_re)
    a_re, a_im = mag * jnp.cos(ld_im), mag * jnp.sin(ld_im)
    den = lam_re * lam_re + lam_im * lam_im
    num_re, num_im = a_re - 1.0, a_im
    coef_re = (num_re * lam_re + num_im * lam_im) / den
    coef_im = (num_im * lam_re - num_re * lam_im) / den
    bb_re = coef_re[..., None] * b_re - coef_im[..., None] * b_im
    bb_im = coef_re[..., None] * b_im + coef_im[..., None] * b_re

    pw_re, pw_im = [jnp.ones_like(a_re)], [jnp.zeros_like(a_im)]
    for _ in range(t):
        r, i = pw_re[-1], pw_im[-1]
        pw_re.append(r * a_re - i * a_im)
        pw_im.append(r * a_im + i * a_re)
    pw_re, pw_im = jnp.stack(pw_re), jnp.stack(pw_im)

    e_re = pw_re[..., None] * bb_re - pw_im[..., None] * bb_im
    e_im = pw_re[..., None] * bb_im + pw_im[..., None] * bb_re
    f_re = c_re[None] * pw_re[:, :, None, :] - c_im[None] * pw_im[:, :, None, :]
    f_im = c_re[None] * pw_im[:, :, None, :] + c_im[None] * pw_re[:, :, None, :]

    cb = (jnp.einsum("gcp,kgpd->kgdc", c_re, e_re[:t], precision=hp)
          - jnp.einsum("gcp,kgpd->kgdc", c_im, e_im[:t], precision=hp))
    cb = cb.at[0].add(jnp.eye(gc, dtype=F32)[None] * d[:, None, :])
    cb = jnp.concatenate([cb, jnp.zeros_like(cb[:1])], axis=0)
    ii = jnp.arange(t)[:, None]
    jj = jnp.arange(t)[None, :]
    lag = jnp.where(jj >= ii, jj - ii, t)
    kmat = cb[lag].transpose(2, 0, 3, 1, 4).reshape(g, t * gc, t * gc)

    rev = t - 1 - jnp.arange(t)
    smat = jnp.concatenate([e_re[rev].transpose(1, 0, 3, 2).reshape(g, t * gc, p),
                            e_im[rev].transpose(1, 0, 3, 2).reshape(g, t * gc, p)], axis=2)
    omat = jnp.concatenate([f_re[1:].transpose(1, 3, 0, 2).reshape(g, p, t * gc),
                            -f_im[1:].transpose(1, 3, 0, 2).reshape(g, p, t * gc)], axis=1)
    at_r = jnp.concatenate([pw_re[t], pw_re[t]], axis=1)[:, None, :]
    at_i = jnp.concatenate([-pw_im[t], pw_im[t]], axis=1)[:, None, :]

    kt = 2 * LANES
    gpt = kt // gc
    n_kt = g // gpt
    eye = jnp.eye(gpt, dtype=F32)

    def bdiag_in(bb):
        x = bb.reshape(n_kt, gpt, p, gc)
        return jnp.einsum("tgpc,gh->tgchp", x, eye).reshape(n_kt, kt, gpt * p)

    def bdiag_out(cc):
        x = cc.reshape(n_kt, gpt, gc, p)
        return jnp.einsum("tgcp,gh->tgphc", x, eye).reshape(n_kt, gpt * p, kt)

    bs = jnp.concatenate([bdiag_in(bb_re), bdiag_in(bb_im)], axis=2)
    cs = jnp.concatenate([bdiag_out(c_re), -bdiag_out(c_im)], axis=1)
    return {"s5_kmat": kmat.astype(BF16), "s5_smat": smat.astype(BF16), "s5_omat": omat.astype(BF16),
            "s5_at_r": at_r, "s5_at_i": at_i,
            "s5_a_re": a_re.reshape(1, g * p), "s5_a_im": a_im.reshape(1, g * p),
            "s5_bs": bs.astype(BF16), "s5_cs": cs.astype(BF16), "s5_d": d.reshape(1, g * gc)}


def _rope_tables(pos, d_rope):
    half = d_rope // 2
    freqs = ROPE_THETA ** (-jnp.arange(half, dtype=F32) / half)
    ang = pos.astype(F32)[:, None] * freqs[None, :]
    cos, sin = jnp.cos(ang), jnp.sin(ang)
    z = jnp.zeros((pos.shape[0], LANES - d_rope), F32)
    zh = jnp.zeros_like(cos)
    return (jnp.concatenate([cos, cos, z], axis=1),
            jnp.concatenate([-sin, zh, z], axis=1),
            jnp.concatenate([zh, sin, z], axis=1))


def _layer_params(l, dims, w_in, norm_attn, mla_q_norm, mla_kv_norm, mla_w_uq, mla_w_uk, mla_w_uv,
                  s5, s5_w_glu, s5_b_glu, pool_w, pool_scale, gmlp_ln_g, gmlp_ln_b, gmlp_w_s, gmlp_b_s,
                  w_branch, w_out):
    r_q, r_kv, d_rope, d_nope, h, w = (dims["Q_LORA"], dims["KV_LORA"], dims["QK_ROPE"], dims["QK_NOPE"],
                                       dims["H"], dims["W"])
    wi = w_in[l]
    k0 = r_q + r_kv
    lp = {"norm": norm_attn[l][None, :],
          "w_in": jnp.concatenate([wi[:, :k0], wi[:, k0 + d_rope:]], axis=1).astype(BF16),
          "w_kpe": jnp.pad(wi[:, k0:k0 + d_rope], ((0, 0), (0, LANES - d_rope))).astype(BF16),
          "q_norm": mla_q_norm[l][None, :], "kv_norm": mla_kv_norm[l][None, :]}
    wuq = mla_w_uq[l]
    lp["w_uq_pad"] = jnp.pad(wuq, ((0, 0), (0, 0), (0, LANES - d_rope))).reshape(r_q, -1).astype(BF16)
    lp["w_uk_flat"] = mla_w_uk[l].reshape(r_kv, -1).astype(BF16)
    lp["w_uv_flat"] = mla_w_uv[l].reshape(r_kv, -1).astype(BF16)
    lp["w_uk_t"] = mla_w_uk[l].transpose(1, 2, 0).astype(BF16)
    lp["w_uv_h"] = mla_w_uv[l].transpose(1, 0, 2).astype(BF16)
    lp.update(s5)
    lp["w_glu"] = s5_w_glu[l].astype(BF16)
    lp["b_glu"] = s5_b_glu[l][None, :]
    lp["pool_w"] = pool_w[l].astype(BF16)
    lp["pool_scale"] = pool_scale[l][None, :]
    lp["ln_g"] = gmlp_ln_g[l][None, :]
    lp["ln_b"] = gmlp_ln_b[l][None, :]
    chunk = gmlp_w_s.shape[2]
    ws = gmlp_w_s[l] * jnp.tril(jnp.ones((chunk, chunk), F32))[None]
    n_g = ws.shape[0]
    lp["gmlp_ws"] = ws.astype(BF16)
    lp["gmlp_bs_t"] = gmlp_b_s[l].T
    lp["gmlp_ws0"] = jnp.repeat(ws[:, 0, 0], w // n_g)[None, :]
    lp["gmlp_bs0"] = jnp.repeat(gmlp_b_s[l][:, 0], w // n_g)[None, :]
    lp["w_branch"] = w_branch[l].astype(BF16)
    lp["w_out"] = w_out[l].astype(BF16)
    return lp


def kernel(x_prompt, x_sample, cache_kv_latent, cache_k_rope, state_s5_re, state_s5_im, state_pool, page_table,
           norm_attn, w_in, mla_q_norm, mla_kv_norm, mla_w_uq, mla_w_uk, mla_w_uv, s5_lambda_re, s5_lambda_im,
           s5_log_dt, s5_b_re, s5_b_im, s5_c_re, s5_c_im, s5_d, s5_w_glu, s5_b_glu, pool_w, pool_scale,
           gmlp_ln_g, gmlp_ln_b, gmlp_w_s, gmlp_b_s, w_branch, w_out, norm_final):
    b, l, d = x_prompt.shape
    ms, dec_seq, _ = x_sample.shape
    assert dec_seq == 1, "one new token per sampled sequence"
    depth = w_in.shape[0]
    w = w_branch.shape[2]
    n_branch = w_branch.shape[1]
    r_q, r_kv = mla_q_norm.shape[1], mla_kv_norm.shape[1]
    h, d_nope, d_v = mla_w_uk.shape[2], mla_w_uk.shape[3], mla_w_uv.shape[3]
    d_rope = mla_w_uq.shape[3] - d_nope
    n_past = page_table.shape[1] * cache_kv_latent.shape[2]
    pool_buf = state_pool.shape[2]
    assert d_nope == LANES and d_v == LANES and d_rope <= LANES and pool_buf == max(POOL_WINDOWS) - 1

    names = ("c_q", "c_kv", "g_mla", "u_s5", "g_s5", "u_pool", "g_pool", "u_gm", "v_gm", "g_gm", "g_merge")
    widths = (r_q, r_kv, w, w, w, w, w, w, w, w, n_branch * d)
    off, acc = {}, 0
    for nm, wd in zip(names, widths):
        off[nm] = acc
        acc += wd
    dims = {"B": b, "L": l, "D": d, "MS": ms, "W": w, "H": h, "QK_NOPE": d_nope, "QK_ROPE": d_rope,
            "V_HEAD": d_v, "Q_LORA": r_q, "KV_LORA": r_kv, "S5_G": s5_b_re.shape[1], "S5_P": s5_b_re.shape[2],
            "S5_GC": s5_b_re.shape[3], "CHUNK": gmlp_w_s.shape[2], "N_PAST": n_past, "off": off,
            "scale": float((d_nope + d_rope) ** -0.5)}

    rope_p = _rope_tables(jnp.arange(l, dtype=jnp.int32), d_rope)
    rope_s = _rope_tables(jnp.full((1,), n_past, dtype=jnp.int32), d_rope)
    final_gain = norm_final[None, :]

    mp = b * l
    hp = x_prompt.reshape(mp, d)
    hs = x_sample.reshape(ms, d)
    tm_p = min(INPROJ_TILE, l)
    tn_in = 1024
    new_p, new_s = [], []
    for layer in range(depth):
        s5 = _s5_params(s5_lambda_re[layer], s5_lambda_im[layer], s5_log_dt[layer], s5_b_re[layer],
                        s5_b_im[layer], s5_c_re[layer], s5_c_im[layer], s5_d[layer])
        lp = _layer_params(layer, dims, w_in, norm_attn, mla_q_norm, mla_kv_norm, mla_w_uq, mla_w_uk, mla_w_uv,
                           s5, s5_w_glu, s5_b_glu, pool_w, pool_scale, gmlp_ln_g, gmlp_ln_b, gmlp_w_s, gmlp_b_s,
                           w_branch, w_out)
        last = layer == depth - 1

        proj, xn = norm_inproj(hp, lp["norm"], lp["w_in"], tm_p, tn_in)
        br_mla, klat, kpe = prompt_mla(proj, xn, lp, dims, rope_p)
        br_s5, s5r, s5i = prompt_s5(proj, lp, dims)
        br_pool = prompt_pool(proj, lp, dims)
        br_gm = prompt_gmlp(proj, lp, dims)
        merged = merge_branches([br_mla, br_s5, br_pool, br_gm], proj, off["g_merge"], lp["w_branch"],
                                min(ROW_TILE, l), 512)
        u_pool = proj[:, off["u_pool"]:off["u_pool"] + w].reshape(b, l, w)
        new_p.append((klat.reshape(b, l, r_kv), kpe.reshape(b, l, d_rope), s5r, s5i, u_pool[:, l - pool_buf:]))
        hp = out_proj(hp, merged, lp["w_out"], min(ROW_TILE, l), final_gain if last else None)

        proj_s, xn_s = norm_inproj(hs, lp["norm"], lp["w_in"], ms, tn_in)
        klat_s, kpe_s, qlat_hm, qpe_hm = sample_qkv(proj_s, xn_s, lp, dims, rope_s)
        olat = sample_decode(page_table, qlat_hm.transpose(1, 0, 2), qpe_hm.transpose(1, 0, 2), klat_s, kpe_s,
                             cache_kv_latent, cache_k_rope, layer, dims)
        g_s, p_s = state_s5_re.shape[2], state_s5_re.shape[3]
        br_s5_s, s5r_s, s5i_s = sample_s5(proj_s, state_s5_re[layer].reshape(ms, g_s * p_s),
                                          state_s5_im[layer].reshape(ms, g_s * p_s), lp, dims)
        past = state_pool[layer]
        br_mla_s, br_pool_s, br_gm_s, v_rows = sample_mix(proj_s, olat.transpose(1, 0, 2),
                                                          past.transpose(1, 0, 2), lp, dims)
        merged_s = merge_branches([br_mla_s, br_s5_s, br_pool_s, br_gm_s], proj_s, off["g_merge"],
                                  lp["w_branch"], ms, 512)
        u_pool_s = proj_s[:, off["u_pool"]:off["u_pool"] + w]
        new_s.append((klat_s.reshape(ms, 1, r_kv), kpe_s.reshape(ms, 1, d_rope),
                      s5r_s.reshape(ms, g_s, p_s), s5i_s.reshape(ms, g_s, p_s),
                      jnp.concatenate([past[:, 1:], u_pool_s[:, None, :]], axis=1),
                      v_rows.reshape(ms, 1, w)))
        hs = out_proj(hs, merged_s, lp["w_out"], ms, final_gain if last else None)

    stack = lambda states, i: jnp.stack([s[i] for s in states], axis=0)
    return (hp.reshape(b, l, d), hs.reshape(ms, 1, d),
            stack(new_p, 0), stack(new_p, 1), stack(new_p, 2), stack(new_p, 3), stack(new_p, 4),
            stack(new_s, 0), stack(new_s, 1), stack(new_s, 2), stack(new_s, 3), stack(new_s, 4),
            stack(new_s, 5))
```

```python
import functools
import math

import jax
import jax.numpy as jnp
from jax import lax
from jax.experimental import pallas as pl
from jax.experimental.pallas import tpu as pltpu

F32 = jnp.float32
BF16 = jnp.bfloat16

NORM_EPS = 1e-6
ROPE_THETA = 10000.0
POOL_WINDOWS = (2, 4, 8, 16)
S5_CHUNK = 16
LANES = 128
VMEM_LIMIT = 56 * 1024 * 1024
GELU_C = math.sqrt(2.0 / math.pi)
ROW_TILE = 512
INPROJ_TILE = 1024
ATTN_TILE = 1024
ATTN_SUB_BLOCKS = 4
LOG2_E = math.log2(math.e)
DECODE_PAGES = 32


def _cparams(*sem):
    return pltpu.CompilerParams(dimension_semantics=sem, vmem_limit_bytes=VMEM_LIMIT)


def _rms(x, g):
    return x * lax.rsqrt(jnp.mean(x * x, axis=-1, keepdims=True) + NORM_EPS) * g


def _sigmoid(x):
    return 1.0 / (1.0 + jnp.exp(-x))


def _silu(x):
    return x * _sigmoid(x)


def _gelu_tanh(x):
    return x * (0.5 * (1.0 + jnp.tanh(GELU_C * (x + 0.044715 * (x * x * x)))))


def _rope128(x, c, s1, s2):
    return x * c + pltpu.roll(x, 96, axis=1) * s1 + pltpu.roll(x, 32, axis=1) * s2


def _dot(a, b):
    return jnp.dot(a, b, preferred_element_type=F32)


def _dot_nt(a, b):
    return lax.dot_general(a, b, (((1,), (1,)), ((), ())), preferred_element_type=F32)


def _norm_inproj_kernel(x_ref, g_ref, w_ref, proj_ref, gate_ref, xn_ref, *, n_proj_tiles):
    j = pl.program_id(1)

    @pl.when(j == 0)
    def _():
        xn_ref[...] = _rms(x_ref[...], g_ref[...]).astype(BF16)

    y = _dot(xn_ref[...], w_ref[...])

    @pl.when(j < n_proj_tiles)
    def _():
        proj_ref[...] = y

    @pl.when(j >= n_proj_tiles)
    def _():
        gate_ref[...] = _sigmoid(y).astype(BF16)


def norm_inproj(x, g, w, n_proj, tm, tn):
    m, d = x.shape
    n = w.shape[1]
    npt = n_proj // tn
    return pl.pallas_call(
        functools.partial(_norm_inproj_kernel, n_proj_tiles=npt),
        grid=(m // tm, n // tn),
        in_specs=[pl.BlockSpec((tm, d), lambda i, j: (i, 0)),
                  pl.BlockSpec((1, d), lambda i, j: (0, 0)),
                  pl.BlockSpec((d, tn), lambda i, j: (0, j))],
        out_specs=[pl.BlockSpec((tm, tn), lambda i, j: (i, jnp.minimum(j, npt - 1))),
                   pl.BlockSpec((tm, tn), lambda i, j: (i, jnp.maximum(j - npt, 0))),
                   pl.BlockSpec((tm, d), lambda i, j: (i, 0))],
        out_shape=[jax.ShapeDtypeStruct((m, n_proj), F32), jax.ShapeDtypeStruct((m, n - n_proj), BF16),
                   jax.ShapeDtypeStruct((m, d), BF16)],
        compiler_params=_cparams("parallel", "arbitrary"),
        name="norm_inproj",
    )(x, g, w)


def _merge_kernel(b0_ref, b1_ref, b2_ref, b3_ref, g0_ref, g1_ref, g2_ref, g3_ref, w_ref, o_ref):
    acc = None
    for k, (b_ref, g_ref) in enumerate(((b0_ref, g0_ref), (b1_ref, g1_ref),
                                        (b2_ref, g2_ref), (b3_ref, g3_ref))):
        t = g_ref[...].astype(F32) * _dot(b_ref[...], w_ref[k])
        acc = t if acc is None else acc + t
    o_ref[...] = acc.astype(BF16)


def merge_branches(branches, gates, w_branch, tm, tn):
    m, w = branches[0].shape
    d = w_branch.shape[2]
    nb = len(branches)
    gate_specs = [pl.BlockSpec((tm, tn), functools.partial(
        lambda i, j, k: (i, k * d // tn + j), k=k)) for k in range(nb)]
    return pl.pallas_call(
        _merge_kernel,
        grid=(m // tm, d // tn),
        in_specs=[pl.BlockSpec((tm, w), lambda i, j: (i, 0))] * nb + gate_specs
        + [pl.BlockSpec((nb, w, tn), lambda i, j: (0, 0, j))],
        out_specs=pl.BlockSpec((tm, tn), lambda i, j: (i, j)),
        out_shape=jax.ShapeDtypeStruct((m, d), BF16),
        compiler_params=_cparams("parallel", "arbitrary"),
        name="merge_branches",
    )(*branches, *([gates] * nb), w_branch)


def _outproj_kernel(h_ref, x_ref, w_ref, o_ref):
    o_ref[...] = h_ref[...] + _dot(x_ref[...], w_ref[...])


def _outproj_norm_kernel(h_ref, x_ref, w_ref, g_ref, o_ref):
    o_ref[...] = _rms(h_ref[...] + _dot(x_ref[...], w_ref[...]), g_ref[...])


def out_proj(h, merged, w_out, tm, final_gain=None):
    m, d = h.shape
    specs = [pl.BlockSpec((tm, d), lambda i: (i, 0)),
             pl.BlockSpec((tm, d), lambda i: (i, 0)),
             pl.BlockSpec((d, d), lambda i: (0, 0))]
    args = [h, merged, w_out]
    body = _outproj_kernel
    if final_gain is not None:
        specs.append(pl.BlockSpec((1, d), lambda i: (0, 0)))
        args.append(final_gain)
        body = _outproj_norm_kernel
    return pl.pallas_call(
        body,
        grid=(m // tm,),
        in_specs=specs,
        out_specs=pl.BlockSpec((tm, d), lambda i: (i, 0)),
        out_shape=jax.ShapeDtypeStruct((m, d), F32),
        compiler_params=_cparams("parallel"),
        name="out_proj",
    )(*args)


def _kv_side_kernel(ckv_ref, xn_ref, wkpe_ref, g_ref, wuk_ref, wuv_ref, c_ref, s1_ref, s2_ref,
                    klat_ref, kpe_ref, kp_ref, v_ref, *, n_heads, d_nope, d_v, d_rope):
    klat = _rms(ckv_ref[...], g_ref[...])
    klat_ref[...] = klat
    kl16 = klat.astype(BF16)
    kpe = _rope128(_dot(xn_ref[...], wkpe_ref[...]), c_ref[...], s1_ref[...], s2_ref[...])
    kpe_ref[...] = kpe[:, :d_rope]
    kpe16 = kpe.astype(BF16)
    knope = _dot(kl16, wuk_ref[...])
    v = _dot(kl16, wuv_ref[...])
    for h in range(n_heads):
        kp_ref[0, h, :, 0:d_nope] = knope[:, h * d_nope:(h + 1) * d_nope].astype(BF16)
        kp_ref[0, h, :, d_nope:d_nope + LANES] = kpe16
        v_ref[0, h] = v[:, h * d_v:(h + 1) * d_v].astype(BF16)


def _q_side_kernel(cq_ref, g_ref, wuq_ref, c_ref, s1_ref, s2_ref, qp_ref, *, n_heads, d_nope, scale):
    cq = _rms(cq_ref[...], g_ref[...]).astype(BF16)
    q = _dot(cq, wuq_ref[...])
    c, s1, s2 = c_ref[...], s1_ref[...], s2_ref[...]
    dq = d_nope + LANES
    for h in range(n_heads):
        qp_ref[0, h, :, 0:d_nope] = (q[:, h * dq:h * dq + d_nope] * scale).astype(BF16)
        pe = _rope128(q[:, h * dq + d_nope:(h + 1) * dq], c, s1, s2)
        qp_ref[0, h, :, d_nope:dq] = (pe * scale).astype(BF16)


def _flash_kernel(q_ref, k_ref, v_ref, g_ref, o_ref, m_sc, l_sc, acc_sc, *, n_sub):
    i = pl.program_id(2)
    j = pl.program_id(3)
    tq = q_ref.shape[2]
    sub = tq // n_sub

    @pl.when(j == 0)
    def _():
        m_sc[...] = jnp.full(m_sc.shape, -jnp.inf, F32)
        l_sc[...] = jnp.zeros(l_sc.shape, F32)
        acc_sc[...] = jnp.zeros(acc_sc.shape, F32)

    def update(diagonal):
        for r in range(n_sub):
            rows = pl.ds(r * sub, sub)
            n_keys = (r + 1) * sub if diagonal else k_ref.shape[2]
            s = _dot_nt(q_ref[0, 0, rows, :], k_ref[0, 0, 0:n_keys, :])
            if diagonal:
                row = lax.broadcasted_iota(jnp.int32, s.shape, 0) + r * sub
                col = lax.broadcasted_iota(jnp.int32, s.shape, 1)
                s = jnp.where(col <= row, s, -jnp.inf)
            m_old = m_sc[rows, :]
            m_new = jnp.maximum(m_old, jnp.max(s, axis=-1, keepdims=True))
            alpha = jnp.exp2(m_old - m_new)
            p = jnp.exp2(s - m_new)
            l_sc[rows, :] = alpha * l_sc[rows, :] + jnp.sum(p, axis=-1, keepdims=True)
            acc_sc[rows, :] = alpha * acc_sc[rows, :] + _dot(p.astype(BF16), v_ref[0, 0, 0:n_keys, :])
            m_sc[rows, :] = m_new

    @pl.when(j < i)
    def _():
        update(False)

    @pl.when(j == i)
    def _():
        update(True)
        o_ref[...] = (acc_sc[...] / l_sc[...] * _silu(g_ref[...])).astype(BF16)


def prompt_mla(proj, xn, lp, dims, rope_tabs):
    b, l, h = dims["B"], dims["L"], dims["H"]
    d_nope, d_v, d_rope = dims["QK_NOPE"], dims["V_HEAD"], dims["QK_ROPE"]
    r_q, r_kv, d = dims["Q_LORA"], dims["KV_LORA"], dims["D"]
    off = dims["off"]
    m = b * l
    tm = min(ROW_TILE, l)
    nl = l // tm
    dk = d_nope + LANES
    c_tab, s1_tab, s2_tab = rope_tabs
    tab_spec = pl.BlockSpec((tm, LANES), lambda i: (i % nl, 0))

    klat, kpe, kp, v = pl.pallas_call(
        functools.partial(_kv_side_kernel, n_heads=h, d_nope=d_nope, d_v=d_v, d_rope=d_rope),
        grid=(m // tm,),
        in_specs=[pl.BlockSpec((tm, r_kv), lambda i: (i, off["c_kv"] // r_kv)),
                  pl.BlockSpec((tm, d), lambda i: (i, 0)),
                  pl.BlockSpec((d, LANES), lambda i: (0, 0)),
                  pl.BlockSpec((1, r_kv), lambda i: (0, 0)),
                  pl.BlockSpec((r_kv, h * d_nope), lambda i: (0, 0)),
                  pl.BlockSpec((r_kv, h * d_v), lambda i: (0, 0)),
                  tab_spec, tab_spec, tab_spec],
        out_specs=[pl.BlockSpec((tm, r_kv), lambda i: (i, 0)),
                   pl.BlockSpec((tm, d_rope), lambda i: (i, 0)),
                   pl.BlockSpec((1, h, tm, dk), lambda i: (i // nl, 0, i % nl, 0)),
                   pl.BlockSpec((1, h, tm, d_v), lambda i: (i // nl, 0, i % nl, 0))],
        out_shape=[jax.ShapeDtypeStruct((m, r_kv), F32),
                   jax.ShapeDtypeStruct((m, d_rope), F32),
                   jax.ShapeDtypeStruct((b, h, l, dk), BF16),
                   jax.ShapeDtypeStruct((b, h, l, d_v), BF16)],
        compiler_params=_cparams("parallel"),
        name="prompt_kv_side",
    )(proj, xn, lp["w_kpe"], lp["kv_norm"], lp["w_uk_flat"], lp["w_uv_flat"], c_tab, s1_tab, s2_tab)

    qp = pl.pallas_call(
        functools.partial(_q_side_kernel, n_heads=h, d_nope=d_nope, scale=dims["scale"] * LOG2_E),
        grid=(m // tm,),
        in_specs=[pl.BlockSpec((tm, r_q), lambda i: (i, off["c_q"] // r_q)),
                  pl.BlockSpec((1, r_q), lambda i: (0, 0)),
                  pl.BlockSpec((r_q, h * dk), lambda i: (0, 0)),
                  tab_spec, tab_spec, tab_spec],
        out_specs=pl.BlockSpec((1, h, tm, dk), lambda i: (i // nl, 0, i % nl, 0)),
        out_shape=jax.ShapeDtypeStruct((b, h, l, dk), BF16),
        compiler_params=_cparams("parallel"),
        name="prompt_q_side",
    )(proj, lp["q_norm"], lp["w_uq_pad"], c_tab, s1_tab, s2_tab)

    tq = min(ATTN_TILE, l)
    nq = l // tq
    br = pl.pallas_call(
        functools.partial(_flash_kernel, n_sub=ATTN_SUB_BLOCKS),
        grid=(b, h, nq, nq),
        in_specs=[pl.BlockSpec((1, 1, tq, dk), lambda bi, hi, i, j: (bi, hi, i, 0)),
                  pl.BlockSpec((1, 1, tq, dk), lambda bi, hi, i, j: (bi, hi, jnp.minimum(i, j), 0)),
                  pl.BlockSpec((1, 1, tq, d_v), lambda bi, hi, i, j: (bi, hi, jnp.minimum(i, j), 0)),
                  pl.BlockSpec((tq, d_v), lambda bi, hi, i, j: (bi * nq + i, off["g_mla"] // d_v + hi))],
        out_specs=pl.BlockSpec((tq, d_v), lambda bi, hi, i, j: (bi * nq + i, hi)),
        out_shape=jax.ShapeDtypeStruct((m, h * d_v), BF16),
        scratch_shapes=[pltpu.VMEM((tq, 1), F32), pltpu.VMEM((tq, 1), F32), pltpu.VMEM((tq, d_v), F32)],
        compiler_params=_cparams("parallel", "parallel", "parallel", "arbitrary"),
        name="prompt_flash",
    )(qp, kp, v, proj)
    return br, klat, kpe


def _cmul(x, pr, pi):
    half = x.shape[1] // 2
    swapped = jnp.concatenate([x[:, half:], x[:, :half]], axis=1)
    return x * pr + swapped * pi


def _s5_chunk_kernel(u_ref, bd_ref, bs_ref, cs_ref, pr_ref, pi_ref, y_ref, hfin_ref, u16_sc, *, t_chunk):
    t = t_chunk
    n = u_ref.shape[0] // t
    x = None
    for i in range(t):
        ui = u_ref[pl.ds(i, n, stride=t), :].astype(BF16)
        u16_sc[i] = ui
        term = _cmul(_dot(ui, bs_ref[0]), pr_ref[0, t - 1 - i:t - i, :], pi_ref[0, t - 1 - i:t - i, :])
        x = term if x is None else x + term

    nidx = lax.broadcasted_iota(jnp.int32, (n, 1), 0)
    ar, ai = pr_ref[0, t:t + 1, :], pi_ref[0, t:t + 1, :]
    d = 1
    while d < n:
        x = x + _cmul(jnp.where(nidx >= d, pltpu.roll(x, d, axis=0), 0.0), ar, ai)
        ar, ai = ar * ar - ai * ai, 2.0 * ar * ai
        d *= 2
    hfin_ref[0, 0] = x[n - 1:n, :]
    h_prev = jnp.where(nidx >= 1, pltpu.roll(x, 1, axis=0), 0.0)

    for j in range(t):
        hj = _cmul(h_prev, pr_ref[0, j + 1:j + 2, :], pi_ref[0, j + 1:j + 2, :]).astype(BF16)
        acc = _dot(hj, cs_ref[0])
        for i in range(j + 1):
            acc = acc + _dot(u16_sc[i], bd_ref[0, j - i])
        y_ref[pl.ds(j, n, stride=t), :] = acc


def _glu_gate_kernel(y_ref, w_ref, b_ref, g_ref, o_ref):
    z = _gelu_tanh(y_ref[...])
    o = z * _sigmoid(_dot(z.astype(BF16), w_ref[...]) + b_ref[...])
    o_ref[...] = (o * _silu(g_ref[...])).astype(BF16)


def prompt_s5(proj, lp, dims):
    b, l, w, off = dims["B"], dims["L"], dims["W"], dims["off"]
    g, p = dims["S5_G"], dims["S5_P"]
    t = S5_CHUNK
    m = b * l
    n_kt, kt, two_ns = lp["s5_bs"].shape
    y, hfin = pl.pallas_call(
        functools.partial(_s5_chunk_kernel, t_chunk=t),
        grid=(b, n_kt),
        in_specs=[pl.BlockSpec((l, kt), lambda bi, gi: (bi, off["u_s5"] // kt + gi)),
                  pl.BlockSpec((1, t, kt, kt), lambda bi, gi: (gi, 0, 0, 0)),
                  pl.BlockSpec((1, kt, two_ns), lambda bi, gi: (gi, 0, 0)),
                  pl.BlockSpec((1, two_ns, kt), lambda bi, gi: (gi, 0, 0)),
                  pl.BlockSpec((1, t + 1, two_ns), lambda bi, gi: (gi, 0, 0)),
                  pl.BlockSpec((1, t + 1, two_ns), lambda bi, gi: (gi, 0, 0))],
        out_specs=[pl.BlockSpec((l, kt), lambda bi, gi: (bi, gi)),
                   pl.BlockSpec((1, 1, 1, two_ns), lambda bi, gi: (bi, gi, 0, 0))],
        out_shape=[jax.ShapeDtypeStruct((m, w), F32),
                   jax.ShapeDtypeStruct((b, n_kt, 1, two_ns), F32)],
        scratch_shapes=[pltpu.VMEM((t, l // t, kt), BF16)],
        compiler_params=_cparams("parallel", "parallel"),
        name="prompt_s5_chunks",
    )(proj, lp["s5_bd"], lp["s5_bs"], lp["s5_cs"], lp["s5_pr"], lp["s5_pi"])
    tm = min(ROW_TILE, l)
    br = pl.pallas_call(
        _glu_gate_kernel,
        grid=(m // tm,),
        in_specs=[pl.BlockSpec((tm, w), lambda i: (i, 0)),
                  pl.BlockSpec((w, w), lambda i: (0, 0)),
                  pl.BlockSpec((1, w), lambda i: (0, 0)),
                  pl.BlockSpec((tm, w), lambda i: (i, off["g_s5"] // w))],
        out_specs=pl.BlockSpec((tm, w), lambda i: (i, 0)),
        out_shape=jax.ShapeDtypeStruct((m, w), BF16),
        compiler_params=_cparams("parallel"),
        name="prompt_glu_gate",
    )(y, lp["w_glu"], lp["b_glu"], proj)
    ns = two_ns // 2
    hfin = hfin.reshape(b, n_kt, two_ns)
    return br, hfin[:, :, :ns].reshape(b, g, p), hfin[:, :, ns:].reshape(b, g, p)


def _pool_kernel(u_ref, halo_ref, g_ref, w_ref, sc_ref, o_ref, *, tiles_per_seq, halo):
    i = pl.program_id(0)
    x = u_ref[...]
    tm, width = x.shape
    gw = width // len(POOL_WINDOWS)
    first = (i % tiles_per_seq) == 0
    hal = jnp.where(first, 0.0, halo_ref[...])
    ext = jnp.concatenate([hal, x], axis=0)
    pos = lax.broadcasted_iota(jnp.int32, (tm, 1), 0) + (i % tiles_per_seq) * tm
    outs = []
    for gi, win in enumerate(POOL_WINDOWS):
        xg = x[:, gi * gw:(gi + 1) * gw]
        eg = ext[:, gi * gw:(gi + 1) * gw]
        s = xg
        for k in range(1, win):
            s = s + pltpu.roll(eg, k, axis=0)[halo:halo + tm]
        cnt = jnp.minimum(pos + 1, win).astype(F32)
        dlt = s / cnt - xg
        outs.append(_dot(dlt.astype(BF16), w_ref[gi]))
    y = jnp.concatenate(outs, axis=1) * sc_ref[...]
    o_ref[...] = (y * _silu(g_ref[...])).astype(BF16)


def _gmlp_kernel(u_ref, v_ref, g_ref, lng_ref, lnb_ref, ws_ref, bs_ref, o_ref, *, chunk):
    v = v_ref[...]
    tm, width = v.shape
    n_groups = ws_ref.shape[0]
    gw = width // n_groups
    mu = jnp.mean(v, axis=-1, keepdims=True)
    vc = v - mu
    vn = vc * lax.rsqrt(jnp.mean(vc * vc, axis=-1, keepdims=True) + NORM_EPS) * lng_ref[...] + lnb_ref[...]
    vn16 = vn.astype(BF16)
    bs = bs_ref[...]
    rows = []
    for c in range(tm // chunk):
        cols = []
        for gi in range(n_groups):
            blk = vn16[c * chunk:(c + 1) * chunk, gi * gw:(gi + 1) * gw]
            cols.append(_dot(ws_ref[gi], blk) + bs[:, gi:gi + 1])
        rows.append(jnp.concatenate(cols, axis=1))
    mixed = jnp.concatenate(rows, axis=0)
    o_ref[...] = (u_ref[...] * mixed * _silu(g_ref[...])).astype(BF16)


def prompt_pool(proj, lp, dims):
    b, l, w, off = dims["B"], dims["L"], dims["W"], dims["off"]
    m = b * l
    tm = min(ROW_TILE, l)
    halo = 16
    nl = l // tm
    n_g = len(POOL_WINDOWS)
    return pl.pallas_call(
        functools.partial(_pool_kernel, tiles_per_seq=nl, halo=halo),
        grid=(m // tm,),
        in_specs=[pl.BlockSpec((tm, w), lambda i: (i, off["u_pool"] // w)),
                  pl.BlockSpec((halo, w), lambda i: (jnp.maximum(i * (tm // halo) - 1, 0), off["u_pool"] // w)),
                  pl.BlockSpec((tm, w), lambda i: (i, off["g_pool"] // w)),
                  pl.BlockSpec((n_g, w // n_g, w // n_g), lambda i: (0, 0, 0)),
                  pl.BlockSpec((1, w), lambda i: (0, 0))],
        out_specs=pl.BlockSpec((tm, w), lambda i: (i, 0)),
        out_shape=jax.ShapeDtypeStruct((m, w), BF16),
        compiler_params=_cparams("parallel"),
        name="prompt_pool",
    )(proj, proj, proj, lp["pool_w"], lp["pool_scale"])


def prompt_gmlp(proj, lp, dims):
    b, l, w, off = dims["B"], dims["L"], dims["W"], dims["off"]
    m = b * l
    chunk = dims["CHUNK"]
    tm = min(ROW_TILE, l)
    n_g = lp["gmlp_ws"].shape[0]
    return pl.pallas_call(
        functools.partial(_gmlp_kernel, chunk=chunk),
        grid=(m // tm,),
        in_specs=[pl.BlockSpec((tm, w), lambda i: (i, off["u_gm"] // w)),
                  pl.BlockSpec((tm, w), lambda i: (i, off["v_gm"] // w)),
                  pl.BlockSpec((tm, w), lambda i: (i, off["g_gm"] // w)),
                  pl.BlockSpec((1, w), lambda i: (0, 0)),
                  pl.BlockSpec((1, w), lambda i: (0, 0)),
                  pl.BlockSpec((n_g, chunk, chunk), lambda i: (0, 0, 0)),
                  pl.BlockSpec((chunk, n_g), lambda i: (0, 0))],
        out_specs=pl.BlockSpec((tm, w), lambda i: (i, 0)),
        out_shape=jax.ShapeDtypeStruct((m, w), BF16),
        compiler_params=_cparams("parallel"),
        name="prompt_gmlp",
    )(proj, proj, proj, lp["ln_g"], lp["ln_b"], lp["gmlp_ws"], lp["gmlp_bs_t"])


def _sample_qkv_kernel(p_ref, xn_ref, wkpe_ref, qg_ref, kg_ref, wuq_ref, wukt_ref, c_ref, s1_ref, s2_ref,
                       klat_ref, kpe_ref, qlat_ref, qpe_ref, *, n_heads, d_nope, r_q, r_kv, d_rope, scale):
    pr = p_ref[...]
    c, s1, s2 = c_ref[...], s1_ref[...], s2_ref[...]
    klat_ref[...] = _rms(pr[:, r_q:r_q + r_kv], kg_ref[...])
    kpe = _rope128(_dot(xn_ref[...], wkpe_ref[...]), c, s1, s2)
    kpe_ref[...] = kpe[:, :d_rope]
    cq = _rms(pr[:, :r_q], qg_ref[...]).astype(BF16)
    q = _dot(cq, wuq_ref[...])
    dq = d_nope + LANES
    for h in range(n_heads):
        qn = q[:, h * dq:h * dq + d_nope].astype(BF16)
        qlat_ref[h] = (_dot(qn, wukt_ref[h]) * scale).astype(BF16)
        qpe_ref[h] = (_rope128(q[:, h * dq + d_nope:(h + 1) * dq], c, s1, s2) * scale).astype(BF16)


def sample_qkv(proj, xn, lp, dims, rope_row):
    ms, h = dims["MS"], dims["H"]
    d_nope, d_rope, r_q, r_kv, d = dims["QK_NOPE"], dims["QK_ROPE"], dims["Q_LORA"], dims["KV_LORA"], dims["D"]
    dk = d_nope + LANES
    full = lambda *shape: pl.BlockSpec(shape, lambda i: (0,) * len(shape))
    return pl.pallas_call(
        functools.partial(_sample_qkv_kernel, n_heads=h, d_nope=d_nope, r_q=r_q, r_kv=r_kv,
                          d_rope=d_rope, scale=dims["scale"]),
        grid=(1,),
        in_specs=[pl.BlockSpec((ms, r_q + r_kv), lambda i: (0, 0)),
                  full(ms, d), full(d, LANES), full(1, r_q), full(1, r_kv), full(r_q, h * dk),
                  full(h, d_nope, r_kv), full(1, LANES), full(1, LANES), full(1, LANES)],
        out_specs=[full(ms, r_kv), full(ms, d_rope), full(h, ms, r_kv), full(h, ms, LANES)],
        out_shape=[jax.ShapeDtypeStruct((ms, r_kv), F32), jax.ShapeDtypeStruct((ms, d_rope), F32),
                   jax.ShapeDtypeStruct((h, ms, r_kv), BF16), jax.ShapeDtypeStruct((h, ms, LANES), BF16)],
        compiler_params=_cparams("arbitrary"),
        name="sample_qkv",
    )(proj, xn, lp["w_kpe"], lp["q_norm"], lp["kv_norm"], lp["w_uq_pad"], lp["w_uk_t"], *rope_row)


def _decode_kernel(pt_ref, qlat_ref, qpe_ref, knl_ref, knp_ref, ckv_hbm, ckr_hbm, o_ref,
                   kvbuf, krbuf, sems, m_sc, l_sc, acc_sc, *, layer, pages_per_chunk, n_chunks, page, d_rope):
    b = pl.program_id(0)
    c = pl.program_id(1)
    step = b * n_chunks + c
    n_steps = pl.num_programs(0) * n_chunks
    slot = step % 2

    def copies(bb, cc, sl):
        out = []
        for pi in range(pages_per_chunk):
            pid = pt_ref[bb, cc * pages_per_chunk + pi]
            out.append(pltpu.make_async_copy(ckv_hbm.at[layer, pid],
                                             kvbuf.at[sl, pl.ds(pi * page, page)], sems.at[0, sl]))
            out.append(pltpu.make_async_copy(ckr_hbm.at[layer, pid], krbuf.at[sl, pi], sems.at[1, sl]))
        return out

    @pl.when(step == 0)
    def _():
        for cp in copies(b, c, slot):
            cp.start()

    @pl.when(step + 1 < n_steps)
    def _():
        nxt = step + 1
        for cp in copies(nxt // n_chunks, nxt % n_chunks, 1 - slot):
            cp.start()

    for cp in copies(b, c, slot):
        cp.wait()

    @pl.when(c == 0)
    def _():
        m_sc[...] = jnp.full(m_sc.shape, -jnp.inf, F32)
        l_sc[...] = jnp.zeros(l_sc.shape, F32)
        acc_sc[...] = jnp.zeros(acc_sc.shape, F32)

    qlat = qlat_ref[0]
    qpe = qpe_ref[0][:, :d_rope]
    kv16 = kvbuf[slot].astype(BF16)
    s_pe = jnp.concatenate([_dot(qpe, krbuf[slot, pi].astype(BF16)) for pi in range(pages_per_chunk)], axis=1)
    s = _dot_nt(qlat, kv16) + s_pe
    m_old = m_sc[...]
    m_new = jnp.maximum(m_old, jnp.max(s, axis=-1, keepdims=True))
    alpha = jnp.exp(m_old - m_new)
    p = jnp.exp(s - m_new)
    l_new = alpha * l_sc[...] + jnp.sum(p, axis=-1, keepdims=True)
    acc_new = alpha * acc_sc[...] + _dot(p.astype(BF16), kv16)
    m_sc[...] = m_new
    l_sc[...] = l_new
    acc_sc[...] = acc_new

    @pl.when(c == n_chunks - 1)
    def _():
        knl = knl_ref[0]
        s_new = (jnp.sum(qlat.astype(F32) * knl, axis=-1, keepdims=True)
                 + jnp.sum(qpe.astype(F32) * knp_ref[0], axis=-1, keepdims=True))
        m_fin = jnp.maximum(m_new, s_new)
        a_old = jnp.exp(m_new - m_fin)
        p_new = jnp.exp(s_new - m_fin)
        o_ref[0] = (a_old * acc_new + p_new * knl) / (a_old * l_new + p_new)


def sample_decode(page_table, qlat, qpe, k_new_lat, k_new_pe, cache_kv, cache_kr, layer, dims):
    ms, h, r_kv, d_rope = dims["MS"], dims["H"], dims["KV_LORA"], dims["QK_ROPE"]
    n_pages = page_table.shape[1]
    page = cache_kv.shape[2]
    ppc = min(DECODE_PAGES, n_pages)
    n_chunks = n_pages // ppc
    grid_spec = pltpu.PrefetchScalarGridSpec(
        num_scalar_prefetch=1,
        grid=(ms, n_chunks),
        in_specs=[pl.BlockSpec((1, h, r_kv), lambda b, c, pt: (b, 0, 0)),
                  pl.BlockSpec((1, h, LANES), lambda b, c, pt: (b, 0, 0)),
                  pl.BlockSpec((1, 1, r_kv), lambda b, c, pt: (b, 0, 0)),
                  pl.BlockSpec((1, 1, d_rope), lambda b, c, pt: (b, 0, 0)),
                  pl.BlockSpec(memory_space=pl.ANY),
                  pl.BlockSpec(memory_space=pl.ANY)],
        out_specs=pl.BlockSpec((1, h, r_kv), lambda b, c, pt: (b, 0, 0)),
        scratch_shapes=[pltpu.VMEM((2, ppc * page, r_kv), F32),
                        pltpu.VMEM((2, ppc, d_rope, page), F32),
                        pltpu.SemaphoreType.DMA((2, 2)),
                        pltpu.VMEM((h, 1), F32), pltpu.VMEM((h, 1), F32), pltpu.VMEM((h, r_kv), F32)])
    return pl.pallas_call(
        functools.partial(_decode_kernel, layer=layer, pages_per_chunk=ppc, n_chunks=n_chunks,
                          page=page, d_rope=d_rope),
        grid_spec=grid_spec,
        out_shape=jax.ShapeDtypeStruct((ms, h, r_kv), F32),
        compiler_params=_cparams("arbitrary", "arbitrary"),
        name="sample_decode",
    )(page_table, qlat, qpe, k_new_lat.reshape(ms, 1, r_kv), k_new_pe.reshape(ms, 1, d_rope), cache_kv, cache_kr)


def _sample_s5_kernel(u_ref, g_ref, h0r_ref, h0i_ref, ar_ref, ai_ref, bs_ref, cs_ref, d_ref, wglu_ref, bglu_ref,
                      br_ref, hr_ref, hi_ref):
    u = u_ref[...]
    u16 = u.astype(BF16)
    n_kt, kt, two_ns = bs_ref.shape
    ns = two_ns // 2
    ys = []
    for t in range(n_kt):
        bu = _dot(u16[:, t * kt:(t + 1) * kt], bs_ref[t])
        sl = slice(t * ns, (t + 1) * ns)
        ar, ai, h0r, h0i = ar_ref[:, sl], ai_ref[:, sl], h0r_ref[:, sl], h0i_ref[:, sl]
        hr = ar * h0r - ai * h0i + bu[:, :ns]
        hi = ar * h0i + ai * h0r + bu[:, ns:]
        hr_ref[:, sl] = hr
        hi_ref[:, sl] = hi
        hcat = jnp.concatenate([hr, hi], axis=1).astype(BF16)
        ys.append(_dot(hcat, cs_ref[t]))
    y = jnp.concatenate(ys, axis=1) + d_ref[...] * u
    z = _gelu_tanh(y)
    o = z * _sigmoid(_dot(z.astype(BF16), wglu_ref[...]) + bglu_ref[...])
    br_ref[...] = (o * _silu(g_ref[...])).astype(BF16)


def sample_s5(proj, h0r, h0i, lp, dims):
    ms, w, off = dims["MS"], dims["W"], dims["off"]
    n_state = h0r.shape[1]
    tr = min(64, ms)
    n_kt, kt, two_ns = lp["s5_bs"].shape
    row = lambda width, cb=0: pl.BlockSpec((tr, width), lambda i: (i, cb))
    full = lambda *shape: pl.BlockSpec(shape, lambda i: (0,) * len(shape))
    return pl.pallas_call(
        _sample_s5_kernel,
        grid=(ms // tr,),
        in_specs=[row(w, off["u_s5"] // w), row(w, off["g_s5"] // w), row(n_state), row(n_state),
                  full(1, n_state), full(1, n_state), full(n_kt, kt, two_ns), full(n_kt, two_ns, kt),
                  full(1, w), full(w, w), full(1, w)],
        out_specs=[row(w), row(n_state), row(n_state)],
        out_shape=[jax.ShapeDtypeStruct((ms, w), BF16), jax.ShapeDtypeStruct((ms, n_state), F32),
                   jax.ShapeDtypeStruct((ms, n_state), F32)],
        compiler_params=_cparams("parallel"),
        name="sample_s5",
    )(proj, proj, h0r, h0i, lp["s5_a_re"], lp["s5_a_im"], lp["s5_bs"], lp["s5_cs"], lp["s5_d"],
      lp["w_glu"], lp["b_glu"])


def _sample_mix_kernel(olat_ref, wuv_ref, gmla_ref, up_ref, gp_ref, past_ref, pw_ref, psc_ref,
                       ug_ref, vg_ref, gg_ref, lng_ref, lnb_ref, ws0_ref, bs0_ref,
                       bmla_ref, bpool_ref, bgm_ref, vrow_ref, *, n_heads, d_v, pool_cnt):
    gm = gmla_ref[...]
    for h in range(n_heads):
        o = _dot(olat_ref[h].astype(BF16), wuv_ref[h])
        bmla_ref[:, h * d_v:(h + 1) * d_v] = (o * _silu(gm[:, h * d_v:(h + 1) * d_v])).astype(BF16)

    u = up_ref[...]
    n_past = past_ref.shape[0]
    gw = u.shape[1] // len(POOL_WINDOWS)
    outs = []
    for gi, win in enumerate(POOL_WINDOWS):
        sl = slice(gi * gw, (gi + 1) * gw)
        s = u[:, sl]
        for k in range(1, win):
            s = s + past_ref[n_past - k][:, sl]
        dlt = s / pool_cnt[gi] - u[:, sl]
        outs.append(_dot(dlt.astype(BF16), pw_ref[gi]))
    yp = jnp.concatenate(outs, axis=1) * psc_ref[...]
    bpool_ref[...] = (yp * _silu(gp_ref[...])).astype(BF16)

    v = vg_ref[...]
    mu = jnp.mean(v, axis=-1, keepdims=True)
    vc = v - mu
    vn = vc * lax.rsqrt(jnp.mean(vc * vc, axis=-1, keepdims=True) + NORM_EPS) * lng_ref[...] + lnb_ref[...]
    vrow_ref[...] = vn
    mixed = ws0_ref[...] * vn + bs0_ref[...]
    bgm_ref[...] = (ug_ref[...] * mixed * _silu(gg_ref[...])).astype(BF16)


def sample_mix(proj, olat_hm, past_t, lp, dims):
    ms, w, h, d_v, r_kv, off = dims["MS"], dims["W"], dims["H"], dims["V_HEAD"], dims["KV_LORA"], dims["off"]
    tr = min(64, ms)
    n_past = past_t.shape[0]
    n_g = len(POOL_WINDOWS)
    pool_cnt = tuple(float(min(dims["N_PAST"] + 1, win)) for win in POOL_WINDOWS)
    row = lambda key: pl.BlockSpec((tr, w), lambda i: (i, off[key] // w))
    full = lambda *shape: pl.BlockSpec(shape, lambda i: (0,) * len(shape))
    out_row = pl.BlockSpec((tr, w), lambda i: (i, 0))
    return pl.pallas_call(
        functools.partial(_sample_mix_kernel, n_heads=h, d_v=d_v, pool_cnt=pool_cnt),
        grid=(ms // tr,),
        in_specs=[pl.BlockSpec((h, tr, r_kv), lambda i: (0, i, 0)), full(h, r_kv, d_v), row("g_mla"),
                  row("u_pool"), row("g_pool"), pl.BlockSpec((n_past, tr, w), lambda i: (0, i, 0)),
                  full(n_g, w // n_g, w // n_g), full(1, w),
                  row("u_gm"), row("v_gm"), row("g_gm"), full(1, w), full(1, w), full(1, w), full(1, w)],
        out_specs=[out_row, out_row, out_row, out_row],
        out_shape=[jax.ShapeDtypeStruct((ms, w), BF16), jax.ShapeDtypeStruct((ms, w), BF16),
                   jax.ShapeDtypeStruct((ms, w), BF16), jax.ShapeDtypeStruct((ms, w), F32)],
        compiler_params=_cparams("parallel"),
        name="sample_mix",
    )(olat_hm, lp["w_uv_h"], proj, proj, proj, past_t, lp["pool_w"], lp["pool_scale"],
      proj, proj, proj, lp["ln_g"], lp["ln_b"], lp["gmlp_ws0"], lp["gmlp_bs0"])


def _s5_params(lam_re, lam_im, log_dt, b_re, b_im, c_re, c_im, d):
    hp = lax.Precision.HIGHEST
    g, p, gc = b_re.shape
    t = S5_CHUNK
    dt = jnp.exp(log_dt)[:, None]
    ld_re, ld_im = lam_re * dt, lam_im * dt
    mag = jnp.exp(ld_re)
    a_re, a_im = mag * jnp.cos(ld_im), mag * jnp.sin(ld_im)
    den = lam_re * lam_re + lam_im * lam_im
    num_re, num_im = a_re - 1.0, a_im
    coef_re = (num_re * lam_re + num_im * lam_im) / den
    coef_im = (num_im * lam_re - num_re * lam_im) / den
    bb_re = coef_re[..., None] * b_re - coef_im[..., None] * b_im
    bb_im = coef_re[..., None] * b_im + coef_im[..., None] * b_re

    pw_re, pw_im = [jnp.ones_like(a_re)], [jnp.zeros_like(a_im)]
    for _ in range(t):
        r, i = pw_re[-1], pw_im[-1]
        pw_re.append(r * a_re - i * a_im)
        pw_im.append(r * a_im + i * a_re)
    pw_re, pw_im = jnp.stack(pw_re), jnp.stack(pw_im)

    e_re = pw_re[:t, ..., None] * bb_re - pw_im[:t, ..., None] * bb_im
    e_im = pw_re[:t, ..., None] * bb_im + pw_im[:t, ..., None] * bb_re

    cb = (jnp.einsum("gcp,kgpd->kgdc", c_re, e_re, precision=hp)
          - jnp.einsum("gcp,kgpd->kgdc", c_im, e_im, precision=hp))
    cb = cb.at[0].add(jnp.eye(gc, dtype=F32)[None] * d[:, None, :])

    kt = LANES
    gpt = kt // gc
    n_kt = g // gpt
    eye = jnp.eye(gpt, dtype=F32)
    bd = jnp.einsum("ktgdc,gh->tkgdhc", cb.reshape(t, n_kt, gpt, gc, gc), eye).reshape(n_kt, t, kt, kt)

    def packed(x_re, x_im):
        r = x_re.reshape(t + 1, n_kt, gpt * p).transpose(1, 0, 2)
        i = x_im.reshape(t + 1, n_kt, gpt * p).transpose(1, 0, 2)
        return jnp.concatenate([r, r], axis=2), jnp.concatenate([-i, i], axis=2)

    pr, pi = packed(pw_re, pw_im)

    def bdiag_in(bb):
        x = bb.reshape(n_kt, gpt, p, gc)
        return jnp.einsum("tgpc,gh->tgchp", x, eye).reshape(n_kt, kt, gpt * p)

    def bdiag_out(cc):
        x = cc.reshape(n_kt, gpt, gc, p)
        return jnp.einsum("tgcp,gh->tgphc", x, eye).reshape(n_kt, gpt * p, kt)

    bs = jnp.concatenate([bdiag_in(bb_re), bdiag_in(bb_im)], axis=2)
    cs = jnp.concatenate([bdiag_out(c_re), -bdiag_out(c_im)], axis=1)
    return {"s5_bd": bd.astype(BF16), "s5_pr": pr, "s5_pi": pi,
            "s5_a_re": a_re.reshape(1, g * p), "s5_a_im": a_im.reshape(1, g * p),
            "s5_bs": bs.astype(BF16), "s5_cs": cs.astype(BF16), "s5_d": d.reshape(1, g * gc)}


def _rope_tables(pos, d_rope):
    half = d_rope // 2
    freqs = ROPE_THETA ** (-jnp.arange(half, dtype=F32) / half)
    ang = pos.astype(F32)[:, None] * freqs[None, :]
    cos, sin = jnp.cos(ang), jnp.sin(ang)
    z = jnp.zeros((pos.shape[0], LANES - d_rope), F32)
    zh = jnp.zeros_like(cos)
    return (jnp.concatenate([cos, cos, z], axis=1),
            jnp.concatenate([-sin, zh, z], axis=1),
            jnp.concatenate([zh, sin, z], axis=1))


def _layer_params(l, dims, w_in, norm_attn, mla_q_norm, mla_kv_norm, mla_w_uq, mla_w_uk, mla_w_uv,
                  s5, s5_w_glu, s5_b_glu, pool_w, pool_scale, gmlp_ln_g, gmlp_ln_b, gmlp_w_s, gmlp_b_s,
                  w_branch, w_out):
    r_q, r_kv, d_rope, d_nope, h, w = (dims["Q_LORA"], dims["KV_LORA"], dims["QK_ROPE"], dims["QK_NOPE"],
                                       dims["H"], dims["W"])
    wi = w_in[l]
    k0 = r_q + r_kv
    lp = {"norm": norm_attn[l][None, :],
          "w_in": jnp.concatenate([wi[:, :k0], wi[:, k0 + d_rope:]], axis=1).astype(BF16),
          "w_kpe": jnp.pad(wi[:, k0:k0 + d_rope], ((0, 0), (0, LANES - d_rope))).astype(BF16),
          "q_norm": mla_q_norm[l][None, :], "kv_norm": mla_kv_norm[l][None, :]}
    wuq = mla_w_uq[l]
    lp["w_uq_pad"] = jnp.pad(wuq, ((0, 0), (0, 0), (0, LANES - d_rope))).reshape(r_q, -1).astype(BF16)
    lp["w_uk_flat"] = mla_w_uk[l].reshape(r_kv, -1).astype(BF16)
    lp["w_uv_flat"] = mla_w_uv[l].reshape(r_kv, -1).astype(BF16)
    lp["w_uk_t"] = mla_w_uk[l].transpose(1, 2, 0).astype(BF16)
    lp["w_uv_h"] = mla_w_uv[l].transpose(1, 0, 2).astype(BF16)
    lp.update(s5)
    lp["w_glu"] = s5_w_glu[l].astype(BF16)
    lp["b_glu"] = s5_b_glu[l][None, :]
    lp["pool_w"] = pool_w[l].astype(BF16)
    lp["pool_scale"] = pool_scale[l][None, :]
    lp["ln_g"] = gmlp_ln_g[l][None, :]
    lp["ln_b"] = gmlp_ln_b[l][None, :]
    chunk = gmlp_w_s.shape[2]
    ws = gmlp_w_s[l] * jnp.tril(jnp.ones((chunk, chunk), F32))[None]
    n_g = ws.shape[0]
    lp["gmlp_ws"] = ws.astype(BF16)
    lp["gmlp_bs_t"] = gmlp_b_s[l].T
    lp["gmlp_ws0"] = jnp.repeat(ws[:, 0, 0], w // n_g)[None, :]
    lp["gmlp_bs0"] = jnp.repeat(gmlp_b_s[l][:, 0], w // n_g)[None, :]
    lp["w_branch"] = w_branch[l].astype(BF16)
    lp["w_out"] = w_out[l].astype(BF16)
    return lp


def kernel(x_prompt, x_sample, cache_kv_latent, cache_k_rope, state_s5_re, state_s5_im, state_pool, page_table,
           norm_attn, w_in, mla_q_norm, mla_kv_norm, mla_w_uq, mla_w_uk, mla_w_uv, s5_lambda_re, s5_lambda_im,
           s5_log_dt, s5_b_re, s5_b_im, s5_c_re, s5_c_im, s5_d, s5_w_glu, s5_b_glu, pool_w, pool_scale,
           gmlp_ln_g, gmlp_ln_b, gmlp_w_s, gmlp_b_s, w_branch, w_out, norm_final):
    b, l, d = x_prompt.shape
    ms, dec_seq, _ = x_sample.shape
    assert dec_seq == 1, "one new token per sampled sequence"
    depth = w_in.shape[0]
    w = w_branch.shape[2]
    n_branch = w_branch.shape[1]
    r_q, r_kv = mla_q_norm.shape[1], mla_kv_norm.shape[1]
    h, d_nope, d_v = mla_w_uk.shape[2], mla_w_uk.shape[3], mla_w_uv.shape[3]
    d_rope = mla_w_uq.shape[3] - d_nope
    n_past = page_table.shape[1] * cache_kv_latent.shape[2]
    pool_buf = state_pool.shape[2]
    assert d_nope == LANES and d_v == LANES and d_rope <= LANES and pool_buf == max(POOL_WINDOWS) - 1

    names = ("c_q", "c_kv", "g_mla", "u_s5", "g_s5", "u_pool", "g_pool", "u_gm", "v_gm", "g_gm", "g_merge")
    widths = (r_q, r_kv, w, w, w, w, w, w, w, w, n_branch * d)
    off, acc = {}, 0
    for nm, wd in zip(names, widths):
        off[nm] = acc
        acc += wd
    dims = {"B": b, "L": l, "D": d, "MS": ms, "W": w, "H": h, "QK_NOPE": d_nope, "QK_ROPE": d_rope,
            "V_HEAD": d_v, "Q_LORA": r_q, "KV_LORA": r_kv, "S5_G": s5_b_re.shape[1], "S5_P": s5_b_re.shape[2],
            "S5_GC": s5_b_re.shape[3], "CHUNK": gmlp_w_s.shape[2], "N_PAST": n_past, "off": off,
            "scale": float((d_nope + d_rope) ** -0.5)}

    rope_p = _rope_tables(jnp.arange(l, dtype=jnp.int32), d_rope)
    rope_s = _rope_tables(jnp.full((1,), n_past, dtype=jnp.int32), d_rope)
    final_gain = norm_final[None, :]
    cache_kr_t = cache_k_rope.transpose(0, 1, 3, 2)

    mp = b * l
    hp = x_prompt.reshape(mp, d)
    hs = x_sample.reshape(ms, d)
    tm_p = min(INPROJ_TILE, l)
    tn_in = 1024
    new_p, new_s = [], []
    for layer in range(depth):
        s5 = _s5_params(s5_lambda_re[layer], s5_lambda_im[layer], s5_log_dt[layer], s5_b_re[layer],
                        s5_b_im[layer], s5_c_re[layer], s5_c_im[layer], s5_d[layer])
        lp = _layer_params(layer, dims, w_in, norm_attn, mla_q_norm, mla_kv_norm, mla_w_uq, mla_w_uk, mla_w_uv,
                           s5, s5_w_glu, s5_b_glu, pool_w, pool_scale, gmlp_ln_g, gmlp_ln_b, gmlp_w_s, gmlp_b_s,
                           w_branch, w_out)
        last = layer == depth - 1

        proj, gates, xn = norm_inproj(hp, lp["norm"], lp["w_in"], off["g_merge"], tm_p, tn_in)
        br_mla, klat, kpe = prompt_mla(proj, xn, lp, dims, rope_p)
        br_s5, s5r, s5i = prompt_s5(proj, lp, dims)
        br_pool = prompt_pool(proj, lp, dims)
        br_gm = prompt_gmlp(proj, lp, dims)
        merged = merge_branches([br_mla, br_s5, br_pool, br_gm], gates, lp["w_branch"], tm_p, 512)
        pool_tail = proj.reshape(b, l, -1)[:, l - pool_buf:, off["u_pool"]:off["u_pool"] + w]
        new_p.append((klat.reshape(b, l, r_kv), kpe.reshape(b, l, d_rope), s5r, s5i, pool_tail))
        hp = out_proj(hp, merged, lp["w_out"], min(ROW_TILE, l), final_gain if last else None)

        proj_s, gates_s, xn_s = norm_inproj(hs, lp["norm"], lp["w_in"], off["g_merge"], ms, tn_in)
        klat_s, kpe_s, qlat_hm, qpe_hm = sample_qkv(proj_s, xn_s, lp, dims, rope_s)
        olat = sample_decode(page_table, qlat_hm.transpose(1, 0, 2), qpe_hm.transpose(1, 0, 2), klat_s, kpe_s,
                             cache_kv_latent, cache_kr_t, layer, dims)
        g_s, p_s = state_s5_re.shape[2], state_s5_re.shape[3]
        br_s5_s, s5r_s, s5i_s = sample_s5(proj_s, state_s5_re[layer].reshape(ms, g_s * p_s),
                                          state_s5_im[layer].reshape(ms, g_s * p_s), lp, dims)
        past = state_pool[layer]
        br_mla_s, br_pool_s, br_gm_s, v_rows = sample_mix(proj_s, olat.transpose(1, 0, 2),
                                                          past.transpose(1, 0, 2), lp, dims)
        merged_s = merge_branches([br_mla_s, br_s5_s, br_pool_s, br_gm_s], gates_s, lp["w_branch"], ms, 512)
        u_pool_s = proj_s[:, off["u_pool"]:off["u_pool"] + w]
        new_s.append((klat_s.reshape(ms, 1, r_kv), kpe_s.reshape(ms, 1, d_rope),
                      s5r_s.reshape(ms, g_s, p_s), s5i_s.reshape(ms, g_s, p_s),
                      jnp.concatenate([past[:, 1:], u_pool_s[:, None, :]], axis=1),
                      v_rows.reshape(ms, 1, w)))
        hs = out_proj(hs, merged_s, lp["w_out"], ms, final_gain if last else None)

    stack = lambda states, i: jnp.stack([s[i] for s in states], axis=0)
    return (hp.reshape(b, l, d), hs.reshape(ms, 1, d),
            stack(new_p, 0), stack(new_p, 1), stack(new_p, 2), stack(new_p, 3), stack(new_p, 4),
            stack(new_s, 0), stack(new_s, 1), stack(new_s, 2), stack(new_s, 3), stack(new_s, 4),
            stack(new_s, 5))
```

```python
import functools
import math

import jax
import jax.numpy as jnp
from jax import lax
from jax.experimental import pallas as pl
from jax.experimental.pallas import tpu as pltpu

F32 = jnp.float32
BF16 = jnp.bfloat16

NORM_EPS = 1e-6
ROPE_THETA = 10000.0
POOL_WINDOWS = (2, 4, 8, 16)
S5_CHUNK = 16
LANES = 128
VMEM_LIMIT = 56 * 1024 * 1024
GELU_C = math.sqrt(2.0 / math.pi)
ROW_TILE = 512
INPROJ_TILE = 1024
INPROJ_COLS = 1024
ATTN_TILE = 1024
ATTN_CHUNK = 512
LOG2_E = math.log2(math.e)
DECODE_PAGES = 64
GATE_SUB = 256
PACK_ROWS = 1024


def _cparams(*sem):
    return pltpu.CompilerParams(dimension_semantics=sem, vmem_limit_bytes=VMEM_LIMIT)


def _rms(x, g):
    return x * lax.rsqrt(jnp.mean(x * x, axis=-1, keepdims=True) + NORM_EPS) * g


def _sigmoid(x):
    return 1.0 / (1.0 + jnp.exp(-x))


def _silu(x):
    return x * _sigmoid(x)


def _gelu_tanh(x):
    return x * (0.5 * (1.0 + jnp.tanh(GELU_C * (x + 0.044715 * (x * x * x)))))


def _rope128(x, c, s1, s2):
    return x * c + pltpu.roll(x, 96, axis=1) * s1 + pltpu.roll(x, 32, axis=1) * s2


def _dot(a, b):
    return jnp.dot(a, b, preferred_element_type=F32)


def _dot_nt(a, b):
    return lax.dot_general(a, b, (((1,), (1,)), ((), ())), preferred_element_type=F32)


def _norm_inproj_kernel(x_ref, g_ref, w_ref, proj_ref, gate_ref, xn_ref, *, n_proj_tiles):
    j = pl.program_id(1)

    @pl.when(j == 0)
    def _():
        xn_ref[...] = _rms(x_ref[...], g_ref[...]).astype(BF16)

    @pl.when(j < n_proj_tiles)
    def _():
        proj_ref[...] = _dot(xn_ref[...], w_ref[...])

    @pl.when(j >= n_proj_tiles)
    def _():
        tn = w_ref.shape[1]
        for c0 in range(0, tn, GATE_SUB):
            cols = slice(c0, min(c0 + GATE_SUB, tn))
            gate_ref[:, cols] = _sigmoid(_dot(xn_ref[...], w_ref[:, cols])).astype(BF16)


def norm_inproj(x, g, w, n_proj, tm, tn):
    m, d = x.shape
    n = w.shape[1]
    npt = n_proj // tn
    return pl.pallas_call(
        functools.partial(_norm_inproj_kernel, n_proj_tiles=npt),
        grid=(m // tm, n // tn),
        in_specs=[pl.BlockSpec((tm, d), lambda i, j: (i, 0)),
                  pl.BlockSpec((1, d), lambda i, j: (0, 0)),
                  pl.BlockSpec((d, tn), lambda i, j: (0, j))],
        out_specs=[pl.BlockSpec((tm, tn), lambda i, j: (i, jnp.minimum(j, npt - 1))),
                   pl.BlockSpec((tm, tn), lambda i, j: (i, jnp.maximum(j - npt, 0))),
                   pl.BlockSpec((tm, d), lambda i, j: (i, 0))],
        out_shape=[jax.ShapeDtypeStruct((m, n_proj), F32), jax.ShapeDtypeStruct((m, n - n_proj), BF16),
                   jax.ShapeDtypeStruct((m, d), BF16)],
        compiler_params=_cparams("parallel", "arbitrary"),
        name="norm_inproj",
    )(x, g, w)


def _pack_w_in_kernel(a_ref, b_ref, o_ref, *, shift):
    @pl.when(pl.program_id(1) == 0)
    def _():
        o_ref[...] = a_ref[0].astype(BF16)

    @pl.when(pl.program_id(1) > 0)
    def _():
        wide = jnp.concatenate([a_ref[0], b_ref[0]], axis=1)
        o_ref[...] = pltpu.roll(wide, wide.shape[1] - shift, axis=1)[:, :o_ref.shape[1]].astype(BF16)


def pack_w_in(w_in, layer, hole_start, hole, tn):
    _, k, n = w_in.shape
    assert hole_start == tn and hole < LANES
    n_out = n - hole
    tr = min(PACK_ROWS, k)
    lane_blocks = tn // LANES
    last_lane_block = (n - 1) // LANES
    return pl.pallas_call(
        functools.partial(_pack_w_in_kernel, shift=hole),
        grid=(k // tr, n_out // tn),
        in_specs=[pl.BlockSpec((1, tr, tn), lambda r, j: (layer, r, j)),
                  pl.BlockSpec((1, tr, LANES),
                               lambda r, j: (layer, r, jnp.minimum((j + 1) * lane_blocks, last_lane_block)))],
        out_specs=pl.BlockSpec((tr, tn), lambda r, j: (r, j)),
        out_shape=jax.ShapeDtypeStruct((k, n_out), BF16),
        compiler_params=_cparams("parallel", "parallel"),
        name="pack_w_in",
    )(w_in, w_in)


def _merge_kernel(b0_ref, b1_ref, b2_ref, b3_ref, g0_ref, g1_ref, g2_ref, g3_ref, w_ref, o_ref):
    acc = None
    for k, (b_ref, g_ref) in enumerate(((b0_ref, g0_ref), (b1_ref, g1_ref),
                                        (b2_ref, g2_ref), (b3_ref, g3_ref))):
        t = g_ref[...].astype(F32) * _dot(b_ref[...], w_ref[k])
        acc = t if acc is None else acc + t
    o_ref[...] = acc.astype(BF16)


def merge_branches(branches, gates, w_branch, tm, tn):
    m, w = branches[0].shape
    d = w_branch.shape[2]
    nb = len(branches)
    gate_specs = [pl.BlockSpec((tm, tn), functools.partial(
        lambda i, j, k: (i, k * d // tn + j), k=k)) for k in range(nb)]
    return pl.pallas_call(
        _merge_kernel,
        grid=(m // tm, d // tn),
        in_specs=[pl.BlockSpec((tm, w), lambda i, j: (i, 0))] * nb + gate_specs
        + [pl.BlockSpec((nb, w, tn), lambda i, j: (0, 0, j))],
        out_specs=pl.BlockSpec((tm, tn), lambda i, j: (i, j)),
        out_shape=jax.ShapeDtypeStruct((m, d), BF16),
        compiler_params=_cparams("parallel", "arbitrary"),
        name="merge_branches",
    )(*branches, *([gates] * nb), w_branch)


def _outproj_kernel(h_ref, x_ref, w_ref, o_ref):
    o_ref[...] = h_ref[...] + _dot(x_ref[...], w_ref[...])


def _outproj_norm_kernel(h_ref, x_ref, w_ref, g_ref, o_ref):
    o_ref[...] = _rms(h_ref[...] + _dot(x_ref[...], w_ref[...]), g_ref[...])


def out_proj(h, merged, w_out, tm, final_gain=None):
    m, d = h.shape
    specs = [pl.BlockSpec((tm, d), lambda i: (i, 0)),
             pl.BlockSpec((tm, d), lambda i: (i, 0)),
             pl.BlockSpec((d, d), lambda i: (0, 0))]
    args = [h, merged, w_out]
    body = _outproj_kernel
    if final_gain is not None:
        specs.append(pl.BlockSpec((1, d), lambda i: (0, 0)))
        args.append(final_gain)
        body = _outproj_norm_kernel
    return pl.pallas_call(
        body,
        grid=(m // tm,),
        in_specs=specs,
        out_specs=pl.BlockSpec((tm, d), lambda i: (i, 0)),
        out_shape=jax.ShapeDtypeStruct((m, d), F32),
        compiler_params=_cparams("parallel"),
        name="out_proj",
    )(*args)


def _kv_side_kernel(ckv_ref, xn_ref, wkpe_ref, g_ref, wuk_ref, wuv_ref, c_ref, s1_ref, s2_ref,
                    klat_ref, kpe_ref, kp_ref, v_ref, *, n_heads, d_nope, d_v, d_rope):
    klat = _rms(ckv_ref[...], g_ref[...])
    klat_ref[...] = klat
    kl16 = klat.astype(BF16)
    kpe = _rope128(_dot(xn_ref[...], wkpe_ref[...]), c_ref[...], s1_ref[...], s2_ref[...])
    kpe_ref[...] = kpe[:, :d_rope]
    kpe16 = kpe.astype(BF16)
    knope = _dot(kl16, wuk_ref[...])
    v = _dot(kl16, wuv_ref[...])
    for h in range(n_heads):
        kp_ref[0, h, :, 0:d_nope] = knope[:, h * d_nope:(h + 1) * d_nope].astype(BF16)
        kp_ref[0, h, :, d_nope:d_nope + LANES] = kpe16
        v_ref[0, h] = v[:, h * d_v:(h + 1) * d_v].astype(BF16)


def _q_side_kernel(cq_ref, g_ref, wuq_ref, c_ref, s1_ref, s2_ref, qp_ref, *, n_heads, d_nope, scale):
    cq = _rms(cq_ref[...], g_ref[...]).astype(BF16)
    q = _dot(cq, wuq_ref[...])
    c, s1, s2 = c_ref[...], s1_ref[...], s2_ref[...]
    dq = d_nope + LANES
    for h in range(n_heads):
        qp_ref[0, h, :, 0:d_nope] = (q[:, h * dq:h * dq + d_nope] * scale).astype(BF16)
        pe = _rope128(q[:, h * dq + d_nope:(h + 1) * dq], c, s1, s2)
        qp_ref[0, h, :, d_nope:dq] = (pe * scale).astype(BF16)


def _flash_kernel(q_ref, k_ref, v_ref, g_ref, o_ref, m_sc, l_sc, acc_sc, s_sc, *, tc):
    i = pl.program_id(2)
    tq = q_ref.shape[2]
    per_tile = tq // tc
    assert per_tile % 2 == 0
    m_sc[...] = jnp.full(m_sc.shape, -jnp.inf, F32)
    l_sc[...] = jnp.zeros(l_sc.shape, F32)
    acc_sc[...] = jnp.zeros(acc_sc.shape, F32)

    def keys(ref, c):
        return ref[0, 0, pl.ds(pl.multiple_of(c * tc, tc), tc), :]

    def accumulate(s, c, row0, n_rows):
        rows = pl.ds(row0, n_rows)
        m_old = m_sc[rows, :]
        m_new = jnp.maximum(m_old, jnp.max(s, axis=-1, keepdims=True))
        alpha = jnp.exp2(m_old - m_new)
        p = jnp.exp2(s - m_new)
        l_sc[rows, :] = alpha * l_sc[rows, :] + jnp.sum(p, axis=-1, keepdims=True)
        acc_sc[rows, :] = alpha * acc_sc[rows, :] + _dot(p.astype(BF16), keys(v_ref, c))
        m_sc[rows, :] = m_new

    n_full = i * per_tile
    s_sc[0] = _dot_nt(q_ref[0, 0], keys(k_ref, 0))

    def body(t, carry):
        c = 2 * t
        s_sc[1] = _dot_nt(q_ref[0, 0], keys(k_ref, c + 1))
        accumulate(s_sc[0], c, 0, tq)
        s_sc[0] = _dot_nt(q_ref[0, 0], keys(k_ref, c + 2))
        accumulate(s_sc[1], c + 1, 0, tq)
        return carry

    lax.fori_loop(0, n_full // 2, body, 0)

    for d in range(per_tile):
        c = n_full + d
        n_rows = tq - d * tc
        if d == 0:
            s = s_sc[0]
        else:
            s = _dot_nt(q_ref[0, 0, pl.ds(d * tc, n_rows), :], keys(k_ref, c))
        row = lax.broadcasted_iota(jnp.int32, s.shape, 0)
        col = lax.broadcasted_iota(jnp.int32, s.shape, 1)
        accumulate(jnp.where(col <= row, s, -jnp.inf), c, d * tc, n_rows)

    o_ref[...] = (acc_sc[...] / l_sc[...] * _silu(g_ref[...])).astype(BF16)


def prompt_mla(proj, xn, lp, dims, rope_tabs):
    b, l, h = dims["B"], dims["L"], dims["H"]
    d_nope, d_v, d_rope = dims["QK_NOPE"], dims["V_HEAD"], dims["QK_ROPE"]
    r_q, r_kv, d = dims["Q_LORA"], dims["KV_LORA"], dims["D"]
    off = dims["off"]
    m = b * l
    tm = min(ROW_TILE, l)
    nl = l // tm
    dk = d_nope + LANES
    c_tab, s1_tab, s2_tab = rope_tabs
    tab_spec = pl.BlockSpec((tm, LANES), lambda i: (i % nl, 0))

    klat, kpe, kp, v = pl.pallas_call(
        functools.partial(_kv_side_kernel, n_heads=h, d_nope=d_nope, d_v=d_v, d_rope=d_rope),
        grid=(m // tm,),
        in_specs=[pl.BlockSpec((tm, r_kv), lambda i: (i, off["c_kv"] // r_kv)),
                  pl.BlockSpec((tm, d), lambda i: (i, 0)),
                  pl.BlockSpec((d, LANES), lambda i: (0, 0)),
                  pl.BlockSpec((1, r_kv), lambda i: (0, 0)),
                  pl.BlockSpec((r_kv, h * d_nope), lambda i: (0, 0)),
                  pl.BlockSpec((r_kv, h * d_v), lambda i: (0, 0)),
                  tab_spec, tab_spec, tab_spec],
        out_specs=[pl.BlockSpec((tm, r_kv), lambda i: (i, 0)),
                   pl.BlockSpec((tm, d_rope), lambda i: (i, 0)),
                   pl.BlockSpec((1, h, tm, dk), lambda i: (i // nl, 0, i % nl, 0)),
                   pl.BlockSpec((1, h, tm, d_v), lambda i: (i // nl, 0, i % nl, 0))],
        out_shape=[jax.ShapeDtypeStruct((m, r_kv), F32),
                   jax.ShapeDtypeStruct((m, d_rope), F32),
                   jax.ShapeDtypeStruct((b, h, l, dk), BF16),
                   jax.ShapeDtypeStruct((b, h, l, d_v), BF16)],
        compiler_params=_cparams("parallel"),
        name="prompt_kv_side",
    )(proj, xn, lp["w_kpe"], lp["kv_norm"], lp["w_uk_flat"], lp["w_uv_flat"], c_tab, s1_tab, s2_tab)

    qp = pl.pallas_call(
        functools.partial(_q_side_kernel, n_heads=h, d_nope=d_nope, scale=dims["scale"] * LOG2_E),
        grid=(m // tm,),
        in_specs=[pl.BlockSpec((tm, r_q), lambda i: (i, off["c_q"] // r_q)),
                  pl.BlockSpec((1, r_q), lambda i: (0, 0)),
                  pl.BlockSpec((r_q, h * dk), lambda i: (0, 0)),
                  tab_spec, tab_spec, tab_spec],
        out_specs=pl.BlockSpec((1, h, tm, dk), lambda i: (i // nl, 0, i % nl, 0)),
        out_shape=jax.ShapeDtypeStruct((b, h, l, dk), BF16),
        compiler_params=_cparams("parallel"),
        name="prompt_q_side",
    )(proj, lp["q_norm"], lp["w_uq_pad"], c_tab, s1_tab, s2_tab)

    tq = min(ATTN_TILE, l)
    nq = l // tq
    tc = min(ATTN_CHUNK, tq // 2)
    br = pl.pallas_call(
        functools.partial(_flash_kernel, tc=tc),
        grid=(b, h, nq),
        in_specs=[pl.BlockSpec((1, 1, tq, dk), lambda bi, hi, i: (bi, hi, i, 0)),
                  pl.BlockSpec((1, 1, l, dk), lambda bi, hi, i: (bi, hi, 0, 0)),
                  pl.BlockSpec((1, 1, l, d_v), lambda bi, hi, i: (bi, hi, 0, 0)),
                  pl.BlockSpec((tq, d_v), lambda bi, hi, i: (bi * nq + i, off["g_mla"] // d_v + hi))],
        out_specs=pl.BlockSpec((tq, d_v), lambda bi, hi, i: (bi * nq + i, hi)),
        out_shape=jax.ShapeDtypeStruct((m, h * d_v), BF16),
        scratch_shapes=[pltpu.VMEM((tq, 1), F32), pltpu.VMEM((tq, 1), F32), pltpu.VMEM((tq, d_v), F32),
                        pltpu.VMEM((2, tq, tc), F32)],
        compiler_params=_cparams("parallel", "parallel", "parallel"),
        name="prompt_flash",
    )(qp, kp, v, proj)
    return br, klat, kpe


def _cmul(x, pr, pi):
    half = x.shape[1] // 2
    swapped = jnp.concatenate([x[:, half:], x[:, :half]], axis=1)
    return x * pr + swapped * pi


def _s5_chunk_kernel(u_ref, bd_ref, bs_ref, cs_ref, pr_ref, pi_ref, y_ref, hfin_ref, u16_sc, *, t_chunk):
    t = t_chunk
    n = u_ref.shape[0] // t
    x = None
    for i in range(t):
        ui = u_ref[pl.ds(i, n, stride=t), :].astype(BF16)
        u16_sc[i] = ui
        term = _cmul(_dot(ui, bs_ref[0]), pr_ref[0, t - 1 - i:t - i, :], pi_ref[0, t - 1 - i:t - i, :])
        x = term if x is None else x + term

    nidx = lax.broadcasted_iota(jnp.int32, (n, 1), 0)
    ar, ai = pr_ref[0, t:t + 1, :], pi_ref[0, t:t + 1, :]
    d = 1
    while d < n:
        x = x + _cmul(jnp.where(nidx >= d, pltpu.roll(x, d, axis=0), 0.0), ar, ai)
        ar, ai = ar * ar - ai * ai, 2.0 * ar * ai
        d *= 2
    hfin_ref[0, 0] = x[n - 1:n, :]
    h_prev = jnp.where(nidx >= 1, pltpu.roll(x, 1, axis=0), 0.0)

    for j in range(t):
        hj = _cmul(h_prev, pr_ref[0, j + 1:j + 2, :], pi_ref[0, j + 1:j + 2, :]).astype(BF16)
        acc = _dot(hj, cs_ref[0])
        for i in range(j + 1):
            acc = acc + _dot(u16_sc[i], bd_ref[0, j - i])
        y_ref[pl.ds(j, n, stride=t), :] = acc


def _glu_gate_kernel(y_ref, w_ref, b_ref, g_ref, o_ref):
    z = _gelu_tanh(y_ref[...])
    o = z * _sigmoid(_dot(z.astype(BF16), w_ref[...]) + b_ref[...])
    o_ref[...] = (o * _silu(g_ref[...])).astype(BF16)


def prompt_s5(proj, lp, dims):
    b, l, w, off = dims["B"], dims["L"], dims["W"], dims["off"]
    g, p = dims["S5_G"], dims["S5_P"]
    t = S5_CHUNK
    m = b * l
    n_kt, kt, two_ns = lp["s5_bs"].shape
    y, hfin = pl.pallas_call(
        functools.partial(_s5_chunk_kernel, t_chunk=t),
        grid=(b, n_kt),
        in_specs=[pl.BlockSpec((l, kt), lambda bi, gi: (bi, off["u_s5"] // kt + gi)),
                  pl.BlockSpec((1, t, kt, kt), lambda bi, gi: (gi, 0, 0, 0)),
                  pl.BlockSpec((1, kt, two_ns), lambda bi, gi: (gi, 0, 0)),
                  pl.BlockSpec((1, two_ns, kt), lambda bi, gi: (gi, 0, 0)),
                  pl.BlockSpec((1, t + 1, two_ns), lambda bi, gi: (gi, 0, 0)),
                  pl.BlockSpec((1, t + 1, two_ns), lambda bi, gi: (gi, 0, 0))],
        out_specs=[pl.BlockSpec((l, kt), lambda bi, gi: (bi, gi)),
                   pl.BlockSpec((1, 1, 1, two_ns), lambda bi, gi: (bi, gi, 0, 0))],
        out_shape=[jax.ShapeDtypeStruct((m, w), F32),
                   jax.ShapeDtypeStruct((b, n_kt, 1, two_ns), F32)],
        scratch_shapes=[pltpu.VMEM((t, l // t, kt), BF16)],
        compiler_params=_cparams("parallel", "parallel"),
        name="prompt_s5_chunks",
    )(proj, lp["s5_bd"], lp["s5_bs"], lp["s5_cs"], lp["s5_pr"], lp["s5_pi"])
    tm = min(ROW_TILE, l)
    br = pl.pallas_call(
        _glu_gate_kernel,
        grid=(m // tm,),
        in_specs=[pl.BlockSpec((tm, w), lambda i: (i, 0)),
                  pl.BlockSpec((w, w), lambda i: (0, 0)),
                  pl.BlockSpec((1, w), lambda i: (0, 0)),
                  pl.BlockSpec((tm, w), lambda i: (i, off["g_s5"] // w))],
        out_specs=pl.BlockSpec((tm, w), lambda i: (i, 0)),
        out_shape=jax.ShapeDtypeStruct((m, w), BF16),
        compiler_params=_cparams("parallel"),
        name="prompt_glu_gate",
    )(y, lp["w_glu"], lp["b_glu"], proj)
    ns = two_ns // 2
    hfin = hfin.reshape(b, n_kt, two_ns)
    return br, hfin[:, :, :ns].reshape(b, g, p), hfin[:, :, ns:].reshape(b, g, p)


def _pool_kernel(u_ref, halo_ref, g_ref, w_ref, sc_ref, o_ref, *, tiles_per_seq, halo):
    i = pl.program_id(0)
    x = u_ref[...]
    tm, width = x.shape
    gw = width // len(POOL_WINDOWS)
    first = (i % tiles_per_seq) == 0
    hal = jnp.where(first, 0.0, halo_ref[...])
    ext = jnp.concatenate([hal, x], axis=0)
    pos = lax.broadcasted_iota(jnp.int32, (tm, 1), 0) + (i % tiles_per_seq) * tm
    outs = []
    for gi, win in enumerate(POOL_WINDOWS):
        xg = x[:, gi * gw:(gi + 1) * gw]
        eg = ext[:, gi * gw:(gi + 1) * gw]
        s = xg
        for k in range(1, win):
            s = s + pltpu.roll(eg, k, axis=0)[halo:halo + tm]
        cnt = jnp.minimum(pos + 1, win).astype(F32)
        dlt = s / cnt - xg
        outs.append(_dot(dlt.astype(BF16), w_ref[gi]))
    y = jnp.concatenate(outs, axis=1) * sc_ref[...]
    o_ref[...] = (y * _silu(g_ref[...])).astype(BF16)


def _gmlp_kernel(u_ref, v_ref, g_ref, lng_ref, lnb_ref, ws_ref, bs_ref, o_ref, *, chunk):
    v = v_ref[...]
    tm, width = v.shape
    n_groups = ws_ref.shape[0]
    gw = width // n_groups
    mu = jnp.mean(v, axis=-1, keepdims=True)
    vc = v - mu
    vn = vc * lax.rsqrt(jnp.mean(vc * vc, axis=-1, keepdims=True) + NORM_EPS) * lng_ref[...] + lnb_ref[...]
    vn16 = vn.astype(BF16)
    bs = bs_ref[...]
    rows = []
    for c in range(tm // chunk):
        cols = []
        for gi in range(n_groups):
            blk = vn16[c * chunk:(c + 1) * chunk, gi * gw:(gi + 1) * gw]
            cols.append(_dot(ws_ref[gi], blk) + bs[:, gi:gi + 1])
        rows.append(jnp.concatenate(cols, axis=1))
    mixed = jnp.concatenate(rows, axis=0)
    o_ref[...] = (u_ref[...] * mixed * _silu(g_ref[...])).astype(BF16)


def prompt_pool(proj, lp, dims):
    b, l, w, off = dims["B"], dims["L"], dims["W"], dims["off"]
    m = b * l
    tm = min(ROW_TILE, l)
    halo = 16
    nl = l // tm
    n_g = len(POOL_WINDOWS)
    return pl.pallas_call(
        functools.partial(_pool_kernel, tiles_per_seq=nl, halo=halo),
        grid=(m // tm,),
        in_specs=[pl.BlockSpec((tm, w), lambda i: (i, off["u_pool"] // w)),
                  pl.BlockSpec((halo, w), lambda i: (jnp.maximum(i * (tm // halo) - 1, 0), off["u_pool"] // w)),
                  pl.BlockSpec((tm, w), lambda i: (i, off["g_pool"] // w)),
                  pl.BlockSpec((n_g, w // n_g, w // n_g), lambda i: (0, 0, 0)),
                  pl.BlockSpec((1, w), lambda i: (0, 0))],
        out_specs=pl.BlockSpec((tm, w), lambda i: (i, 0)),
        out_shape=jax.ShapeDtypeStruct((m, w), BF16),
        compiler_params=_cparams("parallel"),
        name="prompt_pool",
    )(proj, proj, proj, lp["pool_w"], lp["pool_scale"])


def prompt_gmlp(proj, lp, dims):
    b, l, w, off = dims["B"], dims["L"], dims["W"], dims["off"]
    m = b * l
    chunk = dims["CHUNK"]
    tm = min(ROW_TILE, l)
    n_g = lp["gmlp_ws"].shape[0]
    return pl.pallas_call(
        functools.partial(_gmlp_kernel, chunk=chunk),
        grid=(m // tm,),
        in_specs=[pl.BlockSpec((tm, w), lambda i: (i, off["u_gm"] // w)),
                  pl.BlockSpec((tm, w), lambda i: (i, off["v_gm"] // w)),
                  pl.BlockSpec((tm, w), lambda i: (i, off["g_gm"] // w)),
                  pl.BlockSpec((1, w), lambda i: (0, 0)),
                  pl.BlockSpec((1, w), lambda i: (0, 0)),
                  pl.BlockSpec((n_g, chunk, chunk), lambda i: (0, 0, 0)),
                  pl.BlockSpec((chunk, n_g), lambda i: (0, 0))],
        out_specs=pl.BlockSpec((tm, w), lambda i: (i, 0)),
        out_shape=jax.ShapeDtypeStruct((m, w), BF16),
        compiler_params=_cparams("parallel"),
        name="prompt_gmlp",
    )(proj, proj, proj, lp["ln_g"], lp["ln_b"], lp["gmlp_ws"], lp["gmlp_bs_t"])


def _sample_qkv_kernel(p_ref, xn_ref, wkpe_ref, qg_ref, kg_ref, wuq_ref, wukt_ref, c_ref, s1_ref, s2_ref,
                       klat_ref, kpe_ref, qlat_ref, qpe_ref, *, n_heads, d_nope, r_q, r_kv, d_rope, scale):
    pr = p_ref[...]
    c, s1, s2 = c_ref[...], s1_ref[...], s2_ref[...]
    klat_ref[...] = _rms(pr[:, r_q:r_q + r_kv], kg_ref[...])
    kpe = _rope128(_dot(xn_ref[...], wkpe_ref[...]), c, s1, s2)
    kpe_ref[...] = kpe[:, :d_rope]
    cq = _rms(pr[:, :r_q], qg_ref[...]).astype(BF16)
    q = _dot(cq, wuq_ref[...])
    dq = d_nope + LANES
    for h in range(n_heads):
        qn = q[:, h * dq:h * dq + d_nope].astype(BF16)
        qlat_ref[h] = (_dot(qn, wukt_ref[h]) * scale).astype(BF16)
        qpe_ref[h] = (_rope128(q[:, h * dq + d_nope:(h + 1) * dq], c, s1, s2) * scale).astype(BF16)


def sample_qkv(proj, xn, lp, dims, rope_row):
    ms, h = dims["MS"], dims["H"]
    d_nope, d_rope, r_q, r_kv, d = dims["QK_NOPE"], dims["QK_ROPE"], dims["Q_LORA"], dims["KV_LORA"], dims["D"]
    dk = d_nope + LANES
    full = lambda *shape: pl.BlockSpec(shape, lambda i: (0,) * len(shape))
    return pl.pallas_call(
        functools.partial(_sample_qkv_kernel, n_heads=h, d_nope=d_nope, r_q=r_q, r_kv=r_kv,
                          d_rope=d_rope, scale=dims["scale"]),
        grid=(1,),
        in_specs=[pl.BlockSpec((ms, r_q + r_kv), lambda i: (0, 0)),
                  full(ms, d), full(d, LANES), full(1, r_q), full(1, r_kv), full(r_q, h * dk),
                  full(h, d_nope, r_kv), full(1, LANES), full(1, LANES), full(1, LANES)],
        out_specs=[full(ms, r_kv), full(ms, d_rope), full(h, ms, r_kv), full(h, ms, LANES)],
        out_shape=[jax.ShapeDtypeStruct((ms, r_kv), F32), jax.ShapeDtypeStruct((ms, d_rope), F32),
                   jax.ShapeDtypeStruct((h, ms, r_kv), BF16), jax.ShapeDtypeStruct((h, ms, LANES), BF16)],
        compiler_params=_cparams("arbitrary"),
        name="sample_qkv",
    )(proj, xn, lp["w_kpe"], lp["q_norm"], lp["kv_norm"], lp["w_uq_pad"], lp["w_uk_t"], *rope_row)


def _decode_kernel(pt_ref, qlat_ref, qpe_ref, knl_ref, knp_ref, ckv_hbm, ckr_hbm, o_ref,
                   kvbuf, krbuf, sems, m_sc, l_sc, acc_sc, *, layer, pages_per_chunk, n_chunks, page, d_rope):
    b = pl.program_id(0)
    c = pl.program_id(1)
    step = b * n_chunks + c
    n_steps = pl.num_programs(0) * n_chunks
    slot = step % 2

    def copies(bb, cc, sl):
        out = []
        for pi in range(pages_per_chunk):
            pid = pt_ref[bb, cc * pages_per_chunk + pi]
            out.append(pltpu.make_async_copy(ckv_hbm.at[layer, pid],
                                             kvbuf.at[sl, pl.ds(pi * page, page)], sems.at[0, sl]))
            out.append(pltpu.make_async_copy(ckr_hbm.at[layer, pid], krbuf.at[sl, pi], sems.at[1, sl]))
        return out

    @pl.when(step == 0)
    def _():
        for cp in copies(b, c, slot):
            cp.start()

    @pl.when(step + 1 < n_steps)
    def _():
        nxt = step + 1
        for cp in copies(nxt // n_chunks, nxt % n_chunks, 1 - slot):
            cp.start()

    for cp in copies(b, c, slot):
        cp.wait()

    @pl.when(c == 0)
    def _():
        m_sc[...] = jnp.full(m_sc.shape, -jnp.inf, F32)
        l_sc[...] = jnp.zeros(l_sc.shape, F32)
        acc_sc[...] = jnp.zeros(acc_sc.shape, F32)

    qlat = qlat_ref[0]
    qpe = qpe_ref[0][:, :d_rope]
    kv16 = kvbuf[slot].astype(BF16)
    s_pe = jnp.concatenate([_dot(qpe, krbuf[slot, pi].astype(BF16)) for pi in range(pages_per_chunk)], axis=1)
    s = _dot_nt(qlat, kv16) + s_pe
    m_old = m_sc[...]
    m_new = jnp.maximum(m_old, jnp.max(s, axis=-1, keepdims=True))
    alpha = jnp.exp(m_old - m_new)
    p = jnp.exp(s - m_new)
    l_new = alpha * l_sc[...] + jnp.sum(p, axis=-1, keepdims=True)
    acc_new = alpha * acc_sc[...] + _dot(p.astype(BF16), kv16)
    m_sc[...] = m_new
    l_sc[...] = l_new
    acc_sc[...] = acc_new

    @pl.when(c == n_chunks - 1)
    def _():
        knl = knl_ref[0]
        s_new = (jnp.sum(qlat.astype(F32) * knl, axis=-1, keepdims=True)
                 + jnp.sum(qpe.astype(F32) * knp_ref[0], axis=-1, keepdims=True))
        m_fin = jnp.maximum(m_new, s_new)
        a_old = jnp.exp(m_new - m_fin)
        p_new = jnp.exp(s_new - m_fin)
        o_ref[0] = (a_old * acc_new + p_new * knl) / (a_old * l_new + p_new)


def sample_decode(page_table, qlat, qpe, k_new_lat, k_new_pe, cache_kv, cache_kr, layer, dims):
    ms, h, r_kv, d_rope = dims["MS"], dims["H"], dims["KV_LORA"], dims["QK_ROPE"]
    n_pages = page_table.shape[1]
    page = cache_kv.shape[2]
    ppc = min(DECODE_PAGES, n_pages)
    n_chunks = n_pages // ppc
    grid_spec = pltpu.PrefetchScalarGridSpec(
        num_scalar_prefetch=1,
        grid=(ms, n_chunks),
        in_specs=[pl.BlockSpec((1, h, r_kv), lambda b, c, pt: (b, 0, 0)),
                  pl.BlockSpec((1, h, LANES), lambda b, c, pt: (b, 0, 0)),
                  pl.BlockSpec((1, 1, r_kv), lambda b, c, pt: (b, 0, 0)),
                  pl.BlockSpec((1, 1, d_rope), lambda b, c, pt: (b, 0, 0)),
                  pl.BlockSpec(memory_space=pl.ANY),
                  pl.BlockSpec(memory_space=pl.ANY)],
        out_specs=pl.BlockSpec((1, h, r_kv), lambda b, c, pt: (b, 0, 0)),
        scratch_shapes=[pltpu.VMEM((2, ppc * page, r_kv), F32),
                        pltpu.VMEM((2, ppc, d_rope, page), F32),
                        pltpu.SemaphoreType.DMA((2, 2)),
                        pltpu.VMEM((h, 1), F32), pltpu.VMEM((h, 1), F32), pltpu.VMEM((h, r_kv), F32)])
    return pl.pallas_call(
        functools.partial(_decode_kernel, layer=layer, pages_per_chunk=ppc, n_chunks=n_chunks,
                          page=page, d_rope=d_rope),
        grid_spec=grid_spec,
        out_shape=jax.ShapeDtypeStruct((ms, h, r_kv), F32),
        compiler_params=_cparams("arbitrary", "arbitrary"),
        name="sample_decode",
    )(page_table, qlat, qpe, k_new_lat.reshape(ms, 1, r_kv), k_new_pe.reshape(ms, 1, d_rope), cache_kv, cache_kr)


def _sample_s5_kernel(u_ref, g_ref, h0r_ref, h0i_ref, ar_ref, ai_ref, bs_ref, cs_ref, d_ref, wglu_ref, bglu_ref,
                      br_ref, hr_ref, hi_ref):
    u = u_ref[...]
    u16 = u.astype(BF16)
    n_kt, kt, two_ns = bs_ref.shape
    ns = two_ns // 2
    ys = []
    for t in range(n_kt):
        bu = _dot(u16[:, t * kt:(t + 1) * kt], bs_ref[t])
        sl = slice(t * ns, (t + 1) * ns)
        ar, ai, h0r, h0i = ar_ref[:, sl], ai_ref[:, sl], h0r_ref[:, sl], h0i_ref[:, sl]
        hr = ar * h0r - ai * h0i + bu[:, :ns]
        hi = ar * h0i + ai * h0r + bu[:, ns:]
        hr_ref[:, sl] = hr
        hi_ref[:, sl] = hi
        hcat = jnp.concatenate([hr, hi], axis=1).astype(BF16)
        ys.append(_dot(hcat, cs_ref[t]))
    y = jnp.concatenate(ys, axis=1) + d_ref[...] * u
    z = _gelu_tanh(y)
    o = z * _sigmoid(_dot(z.astype(BF16), wglu_ref[...]) + bglu_ref[...])
    br_ref[...] = (o * _silu(g_ref[...])).astype(BF16)


def sample_s5(proj, h0r, h0i, lp, dims):
    ms, w, off = dims["MS"], dims["W"], dims["off"]
    n_state = h0r.shape[1]
    tr = min(64, ms)
    n_kt, kt, two_ns = lp["s5_bs"].shape
    row = lambda width, cb=0: pl.BlockSpec((tr, width), lambda i: (i, cb))
    full = lambda *shape: pl.BlockSpec(shape, lambda i: (0,) * len(shape))
    return pl.pallas_call(
        _sample_s5_kernel,
        grid=(ms // tr,),
        in_specs=[row(w, off["u_s5"] // w), row(w, off["g_s5"] // w), row(n_state), row(n_state),
                  full(1, n_state), full(1, n_state), full(n_kt, kt, two_ns), full(n_kt, two_ns, kt),
                  full(1, w), full(w, w), full(1, w)],
        out_specs=[row(w), row(n_state), row(n_state)],
        out_shape=[jax.ShapeDtypeStruct((ms, w), BF16), jax.ShapeDtypeStruct((ms, n_state), F32),
                   jax.ShapeDtypeStruct((ms, n_state), F32)],
        compiler_params=_cparams("parallel"),
        name="sample_s5",
    )(proj, proj, h0r, h0i, lp["s5_a_re"], lp["s5_a_im"], lp["s5_bs"], lp["s5_cs"], lp["s5_d"],
      lp["w_glu"], lp["b_glu"])


def _sample_mix_kernel(olat_ref, wuv_ref, gmla_ref, up_ref, gp_ref, past_ref, pw_ref, psc_ref,
                       ug_ref, vg_ref, gg_ref, lng_ref, lnb_ref, ws0_ref, bs0_ref,
                       bmla_ref, bpool_ref, bgm_ref, vrow_ref, *, n_heads, d_v, pool_cnt):
    gm = gmla_ref[...]
    for h in range(n_heads):
        o = _dot(olat_ref[h].astype(BF16), wuv_ref[h])
        bmla_ref[:, h * d_v:(h + 1) * d_v] = (o * _silu(gm[:, h * d_v:(h + 1) * d_v])).astype(BF16)

    u = up_ref[...]
    n_past = past_ref.shape[0]
    gw = u.shape[1] // len(POOL_WINDOWS)
    outs = []
    for gi, win in enumerate(POOL_WINDOWS):
        sl = slice(gi * gw, (gi + 1) * gw)
        s = u[:, sl]
        for k in range(1, win):
            s = s + past_ref[n_past - k][:, sl]
        dlt = s / pool_cnt[gi] - u[:, sl]
        outs.append(_dot(dlt.astype(BF16), pw_ref[gi]))
    yp = jnp.concatenate(outs, axis=1) * psc_ref[...]
    bpool_ref[...] = (yp * _silu(gp_ref[...])).astype(BF16)

    v = vg_ref[...]
    mu = jnp.mean(v, axis=-1, keepdims=True)
    vc = v - mu
    vn = vc * lax.rsqrt(jnp.mean(vc * vc, axis=-1, keepdims=True) + NORM_EPS) * lng_ref[...] + lnb_ref[...]
    vrow_ref[...] = vn
    mixed = ws0_ref[...] * vn + bs0_ref[...]
    bgm_ref[...] = (ug_ref[...] * mixed * _silu(gg_ref[...])).astype(BF16)


def sample_mix(proj, olat_hm, past_t, lp, dims):
    ms, w, h, d_v, r_kv, off = dims["MS"], dims["W"], dims["H"], dims["V_HEAD"], dims["KV_LORA"], dims["off"]
    tr = min(64, ms)
    n_past = past_t.shape[0]
    n_g = len(POOL_WINDOWS)
    pool_cnt = tuple(float(min(dims["N_PAST"] + 1, win)) for win in POOL_WINDOWS)
    row = lambda key: pl.BlockSpec((tr, w), lambda i: (i, off[key] // w))
    full = lambda *shape: pl.BlockSpec(shape, lambda i: (0,) * len(shape))
    out_row = pl.BlockSpec((tr, w), lambda i: (i, 0))
    return pl.pallas_call(
        functools.partial(_sample_mix_kernel, n_heads=h, d_v=d_v, pool_cnt=pool_cnt),
        grid=(ms // tr,),
        in_specs=[pl.BlockSpec((h, tr, r_kv), lambda i: (0, i, 0)), full(h, r_kv, d_v), row("g_mla"),
                  row("u_pool"), row("g_pool"), pl.BlockSpec((n_past, tr, w), lambda i: (0, i, 0)),
                  full(n_g, w // n_g, w // n_g), full(1, w),
                  row("u_gm"), row("v_gm"), row("g_gm"), full(1, w), full(1, w), full(1, w), full(1, w)],
        out_specs=[out_row, out_row, out_row, out_row],
        out_shape=[jax.ShapeDtypeStruct((ms, w), BF16), jax.ShapeDtypeStruct((ms, w), BF16),
                   jax.ShapeDtypeStruct((ms, w), BF16), jax.ShapeDtypeStruct((ms, w), F32)],
        compiler_params=_cparams("parallel"),
        name="sample_mix",
    )(olat_hm, lp["w_uv_h"], proj, proj, proj, past_t, lp["pool_w"], lp["pool_scale"],
      proj, proj, proj, lp["ln_g"], lp["ln_b"], lp["gmlp_ws0"], lp["gmlp_bs0"])


def _s5_params(lam_re, lam_im, log_dt, b_re, b_im, c_re, c_im, d):
    hp = lax.Precision.HIGHEST
    g, p, gc = b_re.shape
    t = S5_CHUNK
    dt = jnp.exp(log_dt)[:, None]
    ld_re, ld_im = lam_re * dt, lam_im * dt
    mag = jnp.exp(ld_re)
    a_re, a_im = mag * jnp.cos(ld_im), mag * jnp.sin(ld_im)
    den = lam_re * lam_re + lam_im * lam_im
    num_re, num_im = a_re - 1.0, a_im
    coef_re = (num_re * lam_re + num_im * lam_im) / den
    coef_im = (num_im * lam_re - num_re * lam_im) / den
    bb_re = coef_re[..., None] * b_re - coef_im[..., None] * b_im
    bb_im = coef_re[..., None] * b_im + coef_im[..., None] * b_re

    pw_re, pw_im = [jnp.ones_like(a_re)], [jnp.zeros_like(a_im)]
    for _ in range(t):
        r, i = pw_re[-1], pw_im[-1]
        pw_re.append(r * a_re - i * a_im)
        pw_im.append(r * a_im + i * a_re)
    pw_re, pw_im = jnp.stack(pw_re), jnp.stack(pw_im)

    e_re = pw_re[:t, ..., None] * bb_re - pw_im[:t, ..., None] * bb_im
    e_im = pw_re[:t, ..., None] * bb_im + pw_im[:t, ..., None] * bb_re

    cb = (jnp.einsum("gcp,kgpd->kgdc", c_re, e_re, precision=hp)
          - jnp.einsum("gcp,kgpd->kgdc", c_im, e_im, precision=hp))
    cb = cb.at[0].add(jnp.eye(gc, dtype=F32)[None] * d[:, None, :])

    kt = LANES
    gpt = kt // gc
    n_kt = g // gpt
    eye = jnp.eye(gpt, dtype=F32)
    tile_rows = jnp.arange(kt)[:, None] // gc
    tile_cols = jnp.arange(kt)[None, :] // gc
    bd = jnp.where(tile_rows == tile_cols, jnp.tile(cb.reshape(t, n_kt, kt, gc), (1, 1, 1, gpt)), 0.0)
    bd = bd.transpose(1, 0, 2, 3)

    def packed(x_re, x_im):
        r = x_re.reshape(t + 1, n_kt, gpt * p).transpose(1, 0, 2)
        i = x_im.reshape(t + 1, n_kt, gpt * p).transpose(1, 0, 2)
        return jnp.concatenate([r, r], axis=2), jnp.concatenate([-i, i], axis=2)

    pr, pi = packed(pw_re, pw_im)

    def bdiag_in(bb):
        x = bb.reshape(n_kt, gpt, p, gc)
        return jnp.einsum("tgpc,gh->tgchp", x, eye).reshape(n_kt, kt, gpt * p)

    def bdiag_out(cc):
        x = cc.reshape(n_kt, gpt, gc, p)
        return jnp.einsum("tgcp,gh->tgphc", x, eye).reshape(n_kt, gpt * p, kt)

    bs = jnp.concatenate([bdiag_in(bb_re), bdiag_in(bb_im)], axis=2)
    cs = jnp.concatenate([bdiag_out(c_re), -bdiag_out(c_im)], axis=1)
    return {"s5_bd": bd.astype(BF16), "s5_pr": pr, "s5_pi": pi,
            "s5_a_re": a_re.reshape(1, g * p), "s5_a_im": a_im.reshape(1, g * p),
            "s5_bs": bs.astype(BF16), "s5_cs": cs.astype(BF16), "s5_d": d.reshape(1, g * gc)}


def _rope_tables(pos, d_rope):
    half = d_rope // 2
    freqs = ROPE_THETA ** (-jnp.arange(half, dtype=F32) / half)
    ang = pos.astype(F32)[:, None] * freqs[None, :]
    cos, sin = jnp.cos(ang), jnp.sin(ang)
    z = jnp.zeros((pos.shape[0], LANES - d_rope), F32)
    zh = jnp.zeros_like(cos)
    return (jnp.concatenate([cos, cos, z], axis=1),
            jnp.concatenate([-sin, zh, z], axis=1),
            jnp.concatenate([zh, sin, z], axis=1))


def _layer_params(l, dims, w_in, norm_attn, mla_q_norm, mla_kv_norm, mla_w_uq, mla_w_uk, mla_w_uv,
                  s5, s5_w_glu, s5_b_glu, pool_w, pool_scale, gmlp_ln_g, gmlp_ln_b, gmlp_w_s, gmlp_b_s,
                  w_branch, w_out):
    r_q, r_kv, d_rope, d_nope, h, w = (dims["Q_LORA"], dims["KV_LORA"], dims["QK_ROPE"], dims["QK_NOPE"],
                                       dims["H"], dims["W"])
    wi = w_in[l]
    k0 = r_q + r_kv
    lp = {"norm": norm_attn[l][None, :],
          "w_in": pack_w_in(w_in, l, k0, d_rope, INPROJ_COLS),
          "w_kpe": jnp.pad(wi[:, k0:k0 + d_rope], ((0, 0), (0, LANES - d_rope))).astype(BF16),
          "q_norm": mla_q_norm[l][None, :], "kv_norm": mla_kv_norm[l][None, :]}
    wuq = mla_w_uq[l]
    lp["w_uq_pad"] = jnp.pad(wuq, ((0, 0), (0, 0), (0, LANES - d_rope))).reshape(r_q, -1).astype(BF16)
    lp["w_uk_flat"] = mla_w_uk[l].reshape(r_kv, -1).astype(BF16)
    lp["w_uv_flat"] = mla_w_uv[l].reshape(r_kv, -1).astype(BF16)
    lp["w_uk_t"] = mla_w_uk[l].transpose(1, 2, 0).astype(BF16)
    lp["w_uv_h"] = mla_w_uv[l].transpose(1, 0, 2).astype(BF16)
    lp.update(s5)
    lp["w_glu"] = s5_w_glu[l].astype(BF16)
    lp["b_glu"] = s5_b_glu[l][None, :]
    lp["pool_w"] = pool_w[l].astype(BF16)
    lp["pool_scale"] = pool_scale[l][None, :]
    lp["ln_g"] = gmlp_ln_g[l][None, :]
    lp["ln_b"] = gmlp_ln_b[l][None, :]
    chunk = gmlp_w_s.shape[2]
    ws = gmlp_w_s[l] * jnp.tril(jnp.ones((chunk, chunk), F32))[None]
    n_g = ws.shape[0]
    lp["gmlp_ws"] = ws.astype(BF16)
    lp["gmlp_bs_t"] = gmlp_b_s[l].T
    lp["gmlp_ws0"] = jnp.repeat(ws[:, 0, 0], w // n_g)[None, :]
    lp["gmlp_bs0"] = jnp.repeat(gmlp_b_s[l][:, 0], w // n_g)[None, :]
    lp["w_branch"] = w_branch[l].astype(BF16)
    lp["w_out"] = w_out[l].astype(BF16)
    return lp


def kernel(x_prompt, x_sample, cache_kv_latent, cache_k_rope, state_s5_re, state_s5_im, state_pool, page_table,
           norm_attn, w_in, mla_q_norm, mla_kv_norm, mla_w_uq, mla_w_uk, mla_w_uv, s5_lambda_re, s5_lambda_im,
           s5_log_dt, s5_b_re, s5_b_im, s5_c_re, s5_c_im, s5_d, s5_w_glu, s5_b_glu, pool_w, pool_scale,
           gmlp_ln_g, gmlp_ln_b, gmlp_w_s, gmlp_b_s, w_branch, w_out, norm_final):
    b, l, d = x_prompt.shape
    ms, dec_seq, _ = x_sample.shape
    assert dec_seq == 1, "one new token per sampled sequence"
    depth = w_in.shape[0]
    w = w_branch.shape[2]
    n_branch = w_branch.shape[1]
    r_q, r_kv = mla_q_norm.shape[1], mla_kv_norm.shape[1]
    h, d_nope, d_v = mla_w_uk.shape[2], mla_w_uk.shape[3], mla_w_uv.shape[3]
    d_rope = mla_w_uq.shape[3] - d_nope
    n_past = page_table.shape[1] * cache_kv_latent.shape[2]
    pool_buf = state_pool.shape[2]
    assert d_nope == LANES and d_v == LANES and d_rope <= LANES and pool_buf == max(POOL_WINDOWS) - 1

    names = ("c_q", "c_kv", "g_mla", "u_s5", "g_s5", "u_pool", "g_pool", "u_gm", "v_gm", "g_gm", "g_merge")
    widths = (r_q, r_kv, w, w, w, w, w, w, w, w, n_branch * d)
    off, acc = {}, 0
    for nm, wd in zip(names, widths):
        off[nm] = acc
        acc += wd
    dims = {"B": b, "L": l, "D": d, "MS": ms, "W": w, "H": h, "QK_NOPE": d_nope, "QK_ROPE": d_rope,
            "V_HEAD": d_v, "Q_LORA": r_q, "KV_LORA": r_kv, "S5_G": s5_b_re.shape[1], "S5_P": s5_b_re.shape[2],
            "S5_GC": s5_b_re.shape[3], "CHUNK": gmlp_w_s.shape[2], "N_PAST": n_past, "off": off,
            "scale": float((d_nope + d_rope) ** -0.5)}

    rope_p = _rope_tables(jnp.arange(l, dtype=jnp.int32), d_rope)
    rope_s = _rope_tables(jnp.full((1,), n_past, dtype=jnp.int32), d_rope)
    final_gain = norm_final[None, :]
    cache_kr_t = cache_k_rope.transpose(0, 1, 3, 2)

    mp = b * l
    hp = x_prompt.reshape(mp, d)
    hs = x_sample.reshape(ms, d)
    tm_p = min(INPROJ_TILE, l)
    tn_in = INPROJ_COLS
    new_p, new_s = [], []
    for layer in range(depth):
        s5 = _s5_params(s5_lambda_re[layer], s5_lambda_im[layer], s5_log_dt[layer], s5_b_re[layer],
                        s5_b_im[layer], s5_c_re[layer], s5_c_im[layer], s5_d[layer])
        lp = _layer_params(layer, dims, w_in, norm_attn, mla_q_norm, mla_kv_norm, mla_w_uq, mla_w_uk, mla_w_uv,
                           s5, s5_w_glu, s5_b_glu, pool_w, pool_scale, gmlp_ln_g, gmlp_ln_b, gmlp_w_s, gmlp_b_s,
                           w_branch, w_out)
        last = layer == depth - 1

        proj, gates, xn = norm_inproj(hp, lp["norm"], lp["w_in"], off["g_merge"], tm_p, tn_in)
        br_mla, klat, kpe = prompt_mla(proj, xn, lp, dims, rope_p)
        br_s5, s5r, s5i = prompt_s5(proj, lp, dims)
        br_pool = prompt_pool(proj, lp, dims)
        br_gm = prompt_gmlp(proj, lp, dims)
        merged = merge_branches([br_mla, br_s5, br_pool, br_gm], gates, lp["w_branch"], tm_p, 512)
        pool_tail = proj.reshape(b, l, -1)[:, l - pool_buf:, off["u_pool"]:off["u_pool"] + w]
        new_p.append((klat.reshape(b, l, r_kv), kpe.reshape(b, l, d_rope), s5r, s5i, pool_tail))
        hp = out_proj(hp, merged, lp["w_out"], min(ROW_TILE, l), final_gain if last else None)

        proj_s, gates_s, xn_s = norm_inproj(hs, lp["norm"], lp["w_in"], off["g_merge"], ms, tn_in)
        klat_s, kpe_s, qlat_hm, qpe_hm = sample_qkv(proj_s, xn_s, lp, dims, rope_s)
        olat = sample_decode(page_table, qlat_hm.transpose(1, 0, 2), qpe_hm.transpose(1, 0, 2), klat_s, kpe_s,
                             cache_kv_latent, cache_kr_t, layer, dims)
        g_s, p_s = state_s5_re.shape[2], state_s5_re.shape[3]
        br_s5_s, s5r_s, s5i_s = sample_s5(proj_s, state_s5_re[layer].reshape(ms, g_s * p_s),
                                          state_s5_im[layer].reshape(ms, g_s * p_s), lp, dims)
        past = state_pool[layer]
        br_mla_s, br_pool_s, br_gm_s, v_rows = sample_mix(proj_s, olat.transpose(1, 0, 2),
                                                          past.transpose(1, 0, 2), lp, dims)
        merged_s = merge_branches([br_mla_s, br_s5_s, br_pool_s, br_gm_s], gates_s, lp["w_branch"], ms, 512)
        u_pool_s = proj_s[:, off["u_pool"]:off["u_pool"] + w]
        new_s.append((klat_s.reshape(ms, 1, r_kv), kpe_s.reshape(ms, 1, d_rope),
                      s5r_s.reshape(ms, g_s, p_s), s5i_s.reshape(ms, g_s, p_s),
                      jnp.concatenate([past[:, 1:], u_pool_s[:, None, :]], axis=1),
                      v_rows.reshape(ms, 1, w)))
        hs = out_proj(hs, merged_s, lp["w_out"], ms, final_gain if last else None)

    stack = lambda states, i: jnp.stack([s[i] for s in states], axis=0)
    return (hp.reshape(b, l, d), hs.reshape(ms, 1, d),
            stack(new_p, 0), stack(new_p, 1), stack(new_p, 2), stack(new_p, 3), stack(new_p, 4),
            stack(new_s, 0), stack(new_s, 1), stack(new_s, 2), stack(new_s, 3), stack(new_s, 4),
            stack(new_s, 5))
```

```python
import functools
import math

import jax
import jax.numpy as jnp
from jax import lax
from jax.experimental import pallas as pl
from jax.experimental.pallas import tpu as pltpu

F32 = jnp.float32
BF16 = jnp.bfloat16

NORM_EPS = 1e-6
ROPE_THETA = 10000.0
POOL_WINDOWS = (2, 4, 8, 16)
S5_CHUNK = 16
LANES = 128
VMEM_LIMIT = 56 * 1024 * 1024
GELU_C = math.sqrt(2.0 / math.pi)
ROW_TILE = 512
INPROJ_TILE = 1024
INPROJ_COLS = 1024
ATTN_TILE = 1024
ATTN_CHUNK = 512
LOG2_E = math.log2(math.e)
DECODE_PAGES = 64
GATE_SUB = 256
PACK_ROWS = 1024


def _cparams(*sem):
    return pltpu.CompilerParams(dimension_semantics=sem, vmem_limit_bytes=VMEM_LIMIT)


def _rms(x, g):
    return x * lax.rsqrt(jnp.mean(x * x, axis=-1, keepdims=True) + NORM_EPS) * g


def _sigmoid(x):
    return 1.0 / (1.0 + jnp.exp(-x))


def _silu(x):
    return x * _sigmoid(x)


def _gelu_tanh(x):
    return x * (0.5 * (1.0 + jnp.tanh(GELU_C * (x + 0.044715 * (x * x * x)))))


def _rope128(x, c, s1, s2):
    return x * c + pltpu.roll(x, 96, axis=1) * s1 + pltpu.roll(x, 32, axis=1) * s2


def _dot(a, b):
    return jnp.dot(a, b, preferred_element_type=F32)


def _dot_nt(a, b):
    return lax.dot_general(a, b, (((1,), (1,)), ((), ())), preferred_element_type=F32)


def _norm_inproj_kernel(x_ref, g_ref, w_ref, proj_ref, gate_ref, xn_ref, *, n_proj_tiles):
    j = pl.program_id(1)

    @pl.when(j == 0)
    def _():
        xn_ref[...] = _rms(x_ref[...], g_ref[...]).astype(BF16)

    @pl.when(j < n_proj_tiles)
    def _():
        proj_ref[...] = _dot(xn_ref[...], w_ref[...])

    @pl.when(j >= n_proj_tiles)
    def _():
        tn = w_ref.shape[1]
        for c0 in range(0, tn, GATE_SUB):
            cols = slice(c0, min(c0 + GATE_SUB, tn))
            gate_ref[:, cols] = _sigmoid(_dot(xn_ref[...], w_ref[:, cols])).astype(BF16)


def norm_inproj(x, g, w, n_proj, tm, tn):
    m, d = x.shape
    n = w.shape[1]
    npt = n_proj // tn
    return pl.pallas_call(
        functools.partial(_norm_inproj_kernel, n_proj_tiles=npt),
        grid=(m // tm, n // tn),
        in_specs=[pl.BlockSpec((tm, d), lambda i, j: (i, 0)),
                  pl.BlockSpec((1, d), lambda i, j: (0, 0)),
                  pl.BlockSpec((d, tn), lambda i, j: (0, j))],
        out_specs=[pl.BlockSpec((tm, tn), lambda i, j: (i, jnp.minimum(j, npt - 1))),
                   pl.BlockSpec((tm, tn), lambda i, j: (i, jnp.maximum(j - npt, 0))),
                   pl.BlockSpec((tm, d), lambda i, j: (i, 0))],
        out_shape=[jax.ShapeDtypeStruct((m, n_proj), F32), jax.ShapeDtypeStruct((m, n - n_proj), BF16),
                   jax.ShapeDtypeStruct((m, d), BF16)],
        compiler_params=_cparams("parallel", "arbitrary"),
        name="norm_inproj",
    )(x, g, w)


def _pack_w_in_kernel(a_ref, o_ref):
    o_ref[...] = a_ref[0].T.astype(BF16)


def _pack_w_hole_kernel(a_ref, o_ref):
    a = a_ref[0]
    pad = jnp.zeros((o_ref.shape[1] - a.shape[0], a.shape[1]), F32)
    o_ref[...] = jnp.concatenate([a, pad], axis=0).T.astype(BF16)


def pack_w_in(w_in_t, layer, hole_start, hole, tn):
    _, n, k = w_in_t.shape
    assert hole_start == tn and hole_start % hole == 0 and hole < LANES
    n_out = n - hole
    tk = min(PACK_ROWS, k)
    main = pl.pallas_call(
        _pack_w_in_kernel,
        grid=(n_out // tn, k // tk),
        in_specs=[pl.BlockSpec((pl.Element(1), pl.Element(tn), pl.Element(tk)),
                               lambda j, r: (layer, jnp.where(j == 0, 0, hole // 8 + j * (tn // 8)) * 8, r * tk))],
        out_specs=pl.BlockSpec((tk, tn), lambda j, r: (r, j)),
        out_shape=jax.ShapeDtypeStruct((k, n_out), BF16),
        compiler_params=_cparams("parallel", "parallel"),
        name="pack_w_in",
    )(w_in_t)
    hole_cols = pl.pallas_call(
        _pack_w_hole_kernel,
        grid=(1,),
        in_specs=[pl.BlockSpec((1, hole, k), lambda i: (layer, hole_start // hole, 0))],
        out_specs=pl.BlockSpec((k, LANES), lambda i: (0, 0)),
        out_shape=jax.ShapeDtypeStruct((k, LANES), BF16),
        compiler_params=_cparams("arbitrary"),
        name="pack_w_hole",
    )(w_in_t)
    return main, hole_cols


def _merge_kernel(b0_ref, b1_ref, b2_ref, b3_ref, g0_ref, g1_ref, g2_ref, g3_ref, w_ref, o_ref):
    acc = None
    for k, (b_ref, g_ref) in enumerate(((b0_ref, g0_ref), (b1_ref, g1_ref),
                                        (b2_ref, g2_ref), (b3_ref, g3_ref))):
        t = g_ref[...].astype(F32) * _dot(b_ref[...], w_ref[k])
        acc = t if acc is None else acc + t
    o_ref[...] = acc.astype(BF16)


def merge_branches(branches, gates, w_branch, tm, tn):
    m, w = branches[0].shape
    d = w_branch.shape[2]
    nb = len(branches)
    gate_specs = [pl.BlockSpec((tm, tn), functools.partial(
        lambda i, j, k: (i, k * d // tn + j), k=k)) for k in range(nb)]
    return pl.pallas_call(
        _merge_kernel,
        grid=(m // tm, d // tn),
        in_specs=[pl.BlockSpec((tm, w), lambda i, j: (i, 0))] * nb + gate_specs
        + [pl.BlockSpec((nb, w, tn), lambda i, j: (0, 0, j))],
        out_specs=pl.BlockSpec((tm, tn), lambda i, j: (i, j)),
        out_shape=jax.ShapeDtypeStruct((m, d), BF16),
        compiler_params=_cparams("parallel", "arbitrary"),
        name="merge_branches",
    )(*branches, *([gates] * nb), w_branch)


def _outproj_kernel(h_ref, x_ref, w_ref, o_ref):
    o_ref[...] = h_ref[...] + _dot(x_ref[...], w_ref[...])


def _outproj_norm_kernel(h_ref, x_ref, w_ref, g_ref, o_ref):
    o_ref[...] = _rms(h_ref[...] + _dot(x_ref[...], w_ref[...]), g_ref[...])


def out_proj(h, merged, w_out, tm, final_gain=None):
    m, d = h.shape
    specs = [pl.BlockSpec((tm, d), lambda i: (i, 0)),
             pl.BlockSpec((tm, d), lambda i: (i, 0)),
             pl.BlockSpec((d, d), lambda i: (0, 0))]
    args = [h, merged, w_out]
    body = _outproj_kernel
    if final_gain is not None:
        specs.append(pl.BlockSpec((1, d), lambda i: (0, 0)))
        args.append(final_gain)
        body = _outproj_norm_kernel
    return pl.pallas_call(
        body,
        grid=(m // tm,),
        in_specs=specs,
        out_specs=pl.BlockSpec((tm, d), lambda i: (i, 0)),
        out_shape=jax.ShapeDtypeStruct((m, d), F32),
        compiler_params=_cparams("parallel"),
        name="out_proj",
    )(*args)


def _kv_side_kernel(ckv_ref, xn_ref, wkpe_ref, g_ref, wuk_ref, wuv_ref, c_ref, s1_ref, s2_ref,
                    klat_ref, kpe_ref, kp_ref, v_ref, *, n_heads, d_nope, d_v, d_rope):
    klat = _rms(ckv_ref[...], g_ref[...])
    klat_ref[...] = klat
    kl16 = klat.astype(BF16)
    kpe = _rope128(_dot(xn_ref[...], wkpe_ref[...]), c_ref[...], s1_ref[...], s2_ref[...])
    kpe_ref[...] = kpe[:, :d_rope]
    kpe16 = kpe.astype(BF16)
    knope = _dot(kl16, wuk_ref[...])
    v = _dot(kl16, wuv_ref[...])
    for h in range(n_heads):
        kp_ref[0, h, :, 0:d_nope] = knope[:, h * d_nope:(h + 1) * d_nope].astype(BF16)
        kp_ref[0, h, :, d_nope:d_nope + LANES] = kpe16
        v_ref[0, h] = v[:, h * d_v:(h + 1) * d_v].astype(BF16)


def _q_side_kernel(cq_ref, g_ref, wuq_ref, c_ref, s1_ref, s2_ref, qp_ref, *, n_heads, d_nope, scale):
    cq = _rms(cq_ref[...], g_ref[...]).astype(BF16)
    q = _dot(cq, wuq_ref[...])
    c, s1, s2 = c_ref[...], s1_ref[...], s2_ref[...]
    dq = d_nope + LANES
    for h in range(n_heads):
        qp_ref[0, h, :, 0:d_nope] = (q[:, h * dq:h * dq + d_nope] * scale).astype(BF16)
        pe = _rope128(q[:, h * dq + d_nope:(h + 1) * dq], c, s1, s2)
        qp_ref[0, h, :, d_nope:dq] = (pe * scale).astype(BF16)


def _flash_kernel(q_ref, k_ref, v_ref, g_ref, o_ref, m_sc, l_sc, acc_sc, s_sc, *, tc):
    i = pl.program_id(2)
    tq = q_ref.shape[2]
    per_tile = tq // tc
    assert per_tile % 2 == 0
    m_sc[...] = jnp.full(m_sc.shape, -jnp.inf, F32)
    l_sc[...] = jnp.zeros(l_sc.shape, F32)
    acc_sc[...] = jnp.zeros(acc_sc.shape, F32)

    def keys(ref, c):
        return ref[0, 0, pl.ds(pl.multiple_of(c * tc, tc), tc), :]

    def accumulate(s, c, row0, n_rows):
        rows = pl.ds(row0, n_rows)
        m_old = m_sc[rows, :]
        m_new = jnp.maximum(m_old, jnp.max(s, axis=-1, keepdims=True))
        alpha = jnp.exp2(m_old - m_new)
        p = jnp.exp2(s - m_new)
        l_sc[rows, :] = alpha * l_sc[rows, :] + jnp.sum(p, axis=-1, keepdims=True)
        acc_sc[rows, :] = alpha * acc_sc[rows, :] + _dot(p.astype(BF16), keys(v_ref, c))
        m_sc[rows, :] = m_new

    n_full = i * per_tile
    s_sc[0] = _dot_nt(q_ref[0, 0], keys(k_ref, 0))

    def body(t, carry):
        c = 2 * t
        s_sc[1] = _dot_nt(q_ref[0, 0], keys(k_ref, c + 1))
        accumulate(s_sc[0], c, 0, tq)
        s_sc[0] = _dot_nt(q_ref[0, 0], keys(k_ref, c + 2))
        accumulate(s_sc[1], c + 1, 0, tq)
        return carry

    lax.fori_loop(0, n_full // 2, body, 0)

    for d in range(per_tile):
        c = n_full + d
        n_rows = tq - d * tc
        if d == 0:
            s = s_sc[0]
        else:
            s = _dot_nt(q_ref[0, 0, pl.ds(d * tc, n_rows), :], keys(k_ref, c))
        row = lax.broadcasted_iota(jnp.int32, s.shape, 0)
        col = lax.broadcasted_iota(jnp.int32, s.shape, 1)
        accumulate(jnp.where(col <= row, s, -jnp.inf), c, d * tc, n_rows)

    o_ref[...] = (acc_sc[...] / l_sc[...] * _silu(g_ref[...])).astype(BF16)


def prompt_mla(proj, xn, lp, dims, rope_tabs):
    b, l, h = dims["B"], dims["L"], dims["H"]
    d_nope, d_v, d_rope = dims["QK_NOPE"], dims["V_HEAD"], dims["QK_ROPE"]
    r_q, r_kv, d = dims["Q_LORA"], dims["KV_LORA"], dims["D"]
    off = dims["off"]
    m = b * l
    tm = min(ROW_TILE, l)
    nl = l // tm
    dk = d_nope + LANES
    c_tab, s1_tab, s2_tab = rope_tabs
    tab_spec = pl.BlockSpec((tm, LANES), lambda i: (i % nl, 0))

    klat, kpe, kp, v = pl.pallas_call(
        functools.partial(_kv_side_kernel, n_heads=h, d_nope=d_nope, d_v=d_v, d_rope=d_rope),
        grid=(m // tm,),
        in_specs=[pl.BlockSpec((tm, r_kv), lambda i: (i, off["c_kv"] // r_kv)),
                  pl.BlockSpec((tm, d), lambda i: (i, 0)),
                  pl.BlockSpec((d, LANES), lambda i: (0, 0)),
                  pl.BlockSpec((1, r_kv), lambda i: (0, 0)),
                  pl.BlockSpec((r_kv, h * d_nope), lambda i: (0, 0)),
                  pl.BlockSpec((r_kv, h * d_v), lambda i: (0, 0)),
                  tab_spec, tab_spec, tab_spec],
        out_specs=[pl.BlockSpec((tm, r_kv), lambda i: (i, 0)),
                   pl.BlockSpec((tm, d_rope), lambda i: (i, 0)),
                   pl.BlockSpec((1, h, tm, dk), lambda i: (i // nl, 0, i % nl, 0)),
                   pl.BlockSpec((1, h, tm, d_v), lambda i: (i // nl, 0, i % nl, 0))],
        out_shape=[jax.ShapeDtypeStruct((m, r_kv), F32),
                   jax.ShapeDtypeStruct((m, d_rope), F32),
                   jax.ShapeDtypeStruct((b, h, l, dk), BF16),
                   jax.ShapeDtypeStruct((b, h, l, d_v), BF16)],
        compiler_params=_cparams("parallel"),
        name="prompt_kv_side",
    )(proj, xn, lp["w_kpe"], lp["kv_norm"], lp["w_uk_flat"], lp["w_uv_flat"], c_tab, s1_tab, s2_tab)

    qp = pl.pallas_call(
        functools.partial(_q_side_kernel, n_heads=h, d_nope=d_nope, scale=dims["scale"] * LOG2_E),
        grid=(m // tm,),
        in_specs=[pl.BlockSpec((tm, r_q), lambda i: (i, off["c_q"] // r_q)),
                  pl.BlockSpec((1, r_q), lambda i: (0, 0)),
                  pl.BlockSpec((r_q, h * dk), lambda i: (0, 0)),
                  tab_spec, tab_spec, tab_spec],
        out_specs=pl.BlockSpec((1, h, tm, dk), lambda i: (i // nl, 0, i % nl, 0)),
        out_shape=jax.ShapeDtypeStruct((b, h, l, dk), BF16),
        compiler_params=_cparams("parallel"),
        name="prompt_q_side",
    )(proj, lp["q_norm"], lp["w_uq_pad"], c_tab, s1_tab, s2_tab)

    tq = min(ATTN_TILE, l)
    nq = l // tq
    tc = min(ATTN_CHUNK, tq // 2)
    br = pl.pallas_call(
        functools.partial(_flash_kernel, tc=tc),
        grid=(b, h, nq),
        in_specs=[pl.BlockSpec((1, 1, tq, dk), lambda bi, hi, i: (bi, hi, i, 0)),
                  pl.BlockSpec((1, 1, l, dk), lambda bi, hi, i: (bi, hi, 0, 0)),
                  pl.BlockSpec((1, 1, l, d_v), lambda bi, hi, i: (bi, hi, 0, 0)),
                  pl.BlockSpec((tq, d_v), lambda bi, hi, i: (bi * nq + i, off["g_mla"] // d_v + hi))],
        out_specs=pl.BlockSpec((tq, d_v), lambda bi, hi, i: (bi * nq + i, hi)),
        out_shape=jax.ShapeDtypeStruct((m, h * d_v), BF16),
        scratch_shapes=[pltpu.VMEM((tq, 1), F32), pltpu.VMEM((tq, 1), F32), pltpu.VMEM((tq, d_v), F32),
                        pltpu.VMEM((2, tq, tc), F32)],
        compiler_params=_cparams("parallel", "parallel", "parallel"),
        name="prompt_flash",
    )(qp, kp, v, proj)
    return br, klat, kpe


def _cmul(x, pr, pi):
    half = x.shape[1] // 2
    swapped = jnp.concatenate([x[:, half:], x[:, :half]], axis=1)
    return x * pr + swapped * pi


def _s5_chunk_kernel(u_ref, bd_ref, bs_ref, cs_ref, pr_ref, pi_ref, y_ref, hfin_ref, u16_sc, *, t_chunk):
    t = t_chunk
    n = u_ref.shape[0] // t
    x = None
    for i in range(t):
        ui = u_ref[pl.ds(i, n, stride=t), :].astype(BF16)
        u16_sc[i] = ui
        term = _cmul(_dot(ui, bs_ref[0]), pr_ref[0, t - 1 - i:t - i, :], pi_ref[0, t - 1 - i:t - i, :])
        x = term if x is None else x + term

    nidx = lax.broadcasted_iota(jnp.int32, (n, 1), 0)
    ar, ai = pr_ref[0, t:t + 1, :], pi_ref[0, t:t + 1, :]
    d = 1
    while d < n:
        x = x + _cmul(jnp.where(nidx >= d, pltpu.roll(x, d, axis=0), 0.0), ar, ai)
        ar, ai = ar * ar - ai * ai, 2.0 * ar * ai
        d *= 2
    hfin_ref[0, 0] = x[n - 1:n, :]
    h_prev = jnp.where(nidx >= 1, pltpu.roll(x, 1, axis=0), 0.0)

    for j in range(t):
        hj = _cmul(h_prev, pr_ref[0, j + 1:j + 2, :], pi_ref[0, j + 1:j + 2, :]).astype(BF16)
        acc = _dot(hj, cs_ref[0])
        for i in range(j + 1):
            acc = acc + _dot(u16_sc[i], bd_ref[0, j - i])
        y_ref[pl.ds(j, n, stride=t), :] = acc


def _glu_gate_kernel(y_ref, w_ref, b_ref, g_ref, o_ref):
    z = _gelu_tanh(y_ref[...])
    o = z * _sigmoid(_dot(z.astype(BF16), w_ref[...]) + b_ref[...])
    o_ref[...] = (o * _silu(g_ref[...])).astype(BF16)


def prompt_s5(proj, lp, dims):
    b, l, w, off = dims["B"], dims["L"], dims["W"], dims["off"]
    g, p = dims["S5_G"], dims["S5_P"]
    t = S5_CHUNK
    m = b * l
    n_kt, kt, two_ns = lp["s5_bs"].shape
    y, hfin = pl.pallas_call(
        functools.partial(_s5_chunk_kernel, t_chunk=t),
        grid=(b, n_kt),
        in_specs=[pl.BlockSpec((l, kt), lambda bi, gi: (bi, off["u_s5"] // kt + gi)),
                  pl.BlockSpec((1, t, kt, kt), lambda bi, gi: (gi, 0, 0, 0)),
                  pl.BlockSpec((1, kt, two_ns), lambda bi, gi: (gi, 0, 0)),
                  pl.BlockSpec((1, two_ns, kt), lambda bi, gi: (gi, 0, 0)),
                  pl.BlockSpec((1, t + 1, two_ns), lambda bi, gi: (gi, 0, 0)),
                  pl.BlockSpec((1, t + 1, two_ns), lambda bi, gi: (gi, 0, 0))],
        out_specs=[pl.BlockSpec((l, kt), lambda bi, gi: (bi, gi)),
                   pl.BlockSpec((1, 1, 1, two_ns), lambda bi, gi: (bi, gi, 0, 0))],
        out_shape=[jax.ShapeDtypeStruct((m, w), F32),
                   jax.ShapeDtypeStruct((b, n_kt, 1, two_ns), F32)],
        scratch_shapes=[pltpu.VMEM((t, l // t, kt), BF16)],
        compiler_params=_cparams("parallel", "parallel"),
        name="prompt_s5_chunks",
    )(proj, lp["s5_bd"], lp["s5_bs"], lp["s5_cs"], lp["s5_pr"], lp["s5_pi"])
    tm = min(ROW_TILE, l)
    br = pl.pallas_call(
        _glu_gate_kernel,
        grid=(m // tm,),
        in_specs=[pl.BlockSpec((tm, w), lambda i: (i, 0)),
                  pl.BlockSpec((w, w), lambda i: (0, 0)),
                  pl.BlockSpec((1, w), lambda i: (0, 0)),
                  pl.BlockSpec((tm, w), lambda i: (i, off["g_s5"] // w))],
        out_specs=pl.BlockSpec((tm, w), lambda i: (i, 0)),
        out_shape=jax.ShapeDtypeStruct((m, w), BF16),
        compiler_params=_cparams("parallel"),
        name="prompt_glu_gate",
    )(y, lp["w_glu"], lp["b_glu"], proj)
    ns = two_ns // 2
    hfin = hfin.reshape(b, n_kt, two_ns)
    return br, hfin[:, :, :ns].reshape(b, g, p), hfin[:, :, ns:].reshape(b, g, p)


def _pool_kernel(u_ref, halo_ref, g_ref, w_ref, sc_ref, o_ref, *, tiles_per_seq, halo):
    i = pl.program_id(0)
    x = u_ref[...]
    tm, width = x.shape
    gw = width // len(POOL_WINDOWS)
    first = (i % tiles_per_seq) == 0
    hal = jnp.where(first, 0.0, halo_ref[...])
    ext = jnp.concatenate([hal, x], axis=0)
    pos = lax.broadcasted_iota(jnp.int32, (tm, 1), 0) + (i % tiles_per_seq) * tm
    outs = []
    for gi, win in enumerate(POOL_WINDOWS):
        xg = x[:, gi * gw:(gi + 1) * gw]
        eg = ext[:, gi * gw:(gi + 1) * gw]
        s = xg
        for k in range(1, win):
            s = s + pltpu.roll(eg, k, axis=0)[halo:halo + tm]
        cnt = jnp.minimum(pos + 1, win).astype(F32)
        dlt = s / cnt - xg
        outs.append(_dot(dlt.astype(BF16), w_ref[gi]))
    y = jnp.concatenate(outs, axis=1) * sc_ref[...]
    o_ref[...] = (y * _silu(g_ref[...])).astype(BF16)


def _gmlp_kernel(u_ref, v_ref, g_ref, lng_ref, lnb_ref, ws_ref, bs_ref, o_ref, *, chunk):
    v = v_ref[...]
    tm, width = v.shape
    n_groups = ws_ref.shape[0]
    gw = width // n_groups
    mu = jnp.mean(v, axis=-1, keepdims=True)
    vc = v - mu
    vn = vc * lax.rsqrt(jnp.mean(vc * vc, axis=-1, keepdims=True) + NORM_EPS) * lng_ref[...] + lnb_ref[...]
    vn16 = vn.astype(BF16)
    bs = bs_ref[...]
    rows = []
    for c in range(tm // chunk):
        cols = []
        for gi in range(n_groups):
            blk = vn16[c * chunk:(c + 1) * chunk, gi * gw:(gi + 1) * gw]
            cols.append(_dot(ws_ref[gi], blk) + bs[:, gi:gi + 1])
        rows.append(jnp.concatenate(cols, axis=1))
    mixed = jnp.concatenate(rows, axis=0)
    o_ref[...] = (u_ref[...] * mixed * _silu(g_ref[...])).astype(BF16)


def prompt_pool(proj, lp, dims):
    b, l, w, off = dims["B"], dims["L"], dims["W"], dims["off"]
    m = b * l
    tm = min(ROW_TILE, l)
    halo = 16
    nl = l // tm
    n_g = len(POOL_WINDOWS)
    return pl.pallas_call(
        functools.partial(_pool_kernel, tiles_per_seq=nl, halo=halo),
        grid=(m // tm,),
        in_specs=[pl.BlockSpec((tm, w), lambda i: (i, off["u_pool"] // w)),
                  pl.BlockSpec((halo, w), lambda i: (jnp.maximum(i * (tm // halo) - 1, 0), off["u_pool"] // w)),
                  pl.BlockSpec((tm, w), lambda i: (i, off["g_pool"] // w)),
                  pl.BlockSpec((n_g, w // n_g, w // n_g), lambda i: (0, 0, 0)),
                  pl.BlockSpec((1, w), lambda i: (0, 0))],
        out_specs=pl.BlockSpec((tm, w), lambda i: (i, 0)),
        out_shape=jax.ShapeDtypeStruct((m, w), BF16),
        compiler_params=_cparams("parallel"),
        name="prompt_pool",
    )(proj, proj, proj, lp["pool_w"], lp["pool_scale"])


def prompt_gmlp(proj, lp, dims):
    b, l, w, off = dims["B"], dims["L"], dims["W"], dims["off"]
    m = b * l
    chunk = dims["CHUNK"]
    tm = min(ROW_TILE, l)
    n_g = lp["gmlp_ws"].shape[0]
    return pl.pallas_call(
        functools.partial(_gmlp_kernel, chunk=chunk),
        grid=(m // tm,),
        in_specs=[pl.BlockSpec((tm, w), lambda i: (i, off["u_gm"] // w)),
                  pl.BlockSpec((tm, w), lambda i: (i, off["v_gm"] // w)),
                  pl.BlockSpec((tm, w), lambda i: (i, off["g_gm"] // w)),
                  pl.BlockSpec((1, w), lambda i: (0, 0)),
                  pl.BlockSpec((1, w), lambda i: (0, 0)),
                  pl.BlockSpec((n_g, chunk, chunk), lambda i: (0, 0, 0)),
                  pl.BlockSpec((chunk, n_g), lambda i: (0, 0))],
        out_specs=pl.BlockSpec((tm, w), lambda i: (i, 0)),
        out_shape=jax.ShapeDtypeStruct((m, w), BF16),
        compiler_params=_cparams("parallel"),
        name="prompt_gmlp",
    )(proj, proj, proj, lp["ln_g"], lp["ln_b"], lp["gmlp_ws"], lp["gmlp_bs_t"])


def _sample_qkv_kernel(p_ref, xn_ref, wkpe_ref, qg_ref, kg_ref, wuq_ref, wukt_ref, c_ref, s1_ref, s2_ref,
                       klat_ref, kpe_ref, qlat_ref, qpe_ref, *, n_heads, d_nope, r_q, r_kv, d_rope, scale):
    pr = p_ref[...]
    c, s1, s2 = c_ref[...], s1_ref[...], s2_ref[...]
    klat_ref[...] = _rms(pr[:, r_q:r_q + r_kv], kg_ref[...])
    kpe = _rope128(_dot(xn_ref[...], wkpe_ref[...]), c, s1, s2)
    kpe_ref[...] = kpe[:, :d_rope]
    cq = _rms(pr[:, :r_q], qg_ref[...]).astype(BF16)
    q = _dot(cq, wuq_ref[...])
    dq = d_nope + LANES
    for h in range(n_heads):
        qn = q[:, h * dq:h * dq + d_nope].astype(BF16)
        qlat_ref[h] = (_dot(qn, wukt_ref[h]) * scale).astype(BF16)
        qpe_ref[h] = (_rope128(q[:, h * dq + d_nope:(h + 1) * dq], c, s1, s2) * scale).astype(BF16)


def sample_qkv(proj, xn, lp, dims, rope_row):
    ms, h = dims["MS"], dims["H"]
    d_nope, d_rope, r_q, r_kv, d = dims["QK_NOPE"], dims["QK_ROPE"], dims["Q_LORA"], dims["KV_LORA"], dims["D"]
    dk = d_nope + LANES
    full = lambda *shape: pl.BlockSpec(shape, lambda i: (0,) * len(shape))
    return pl.pallas_call(
        functools.partial(_sample_qkv_kernel, n_heads=h, d_nope=d_nope, r_q=r_q, r_kv=r_kv,
                          d_rope=d_rope, scale=dims["scale"]),
        grid=(1,),
        in_specs=[pl.BlockSpec((ms, r_q + r_kv), lambda i: (0, 0)),
                  full(ms, d), full(d, LANES), full(1, r_q), full(1, r_kv), full(r_q, h * dk),
                  full(h, d_nope, r_kv), full(1, LANES), full(1, LANES), full(1, LANES)],
        out_specs=[full(ms, r_kv), full(ms, d_rope), full(h, ms, r_kv), full(h, ms, LANES)],
        out_shape=[jax.ShapeDtypeStruct((ms, r_kv), F32), jax.ShapeDtypeStruct((ms, d_rope), F32),
                   jax.ShapeDtypeStruct((h, ms, r_kv), BF16), jax.ShapeDtypeStruct((h, ms, LANES), BF16)],
        compiler_params=_cparams("arbitrary"),
        name="sample_qkv",
    )(proj, xn, lp["w_kpe"], lp["q_norm"], lp["kv_norm"], lp["w_uq_pad"], lp["w_uk_t"], *rope_row)


def _decode_kernel(pt_ref, qlat_ref, qpe_ref, knl_ref, knp_ref, ckv_hbm, ckr_hbm, o_ref,
                   kvbuf, krbuf, sems, m_sc, l_sc, acc_sc, *, layer, pages_per_chunk, n_chunks, page, d_rope):
    b = pl.program_id(0)
    c = pl.program_id(1)
    step = b * n_chunks + c
    n_steps = pl.num_programs(0) * n_chunks
    slot = step % 2

    def copies(bb, cc, sl):
        out = []
        for pi in range(pages_per_chunk):
            pid = pt_ref[bb, cc * pages_per_chunk + pi]
            out.append(pltpu.make_async_copy(ckv_hbm.at[layer, pid],
                                             kvbuf.at[sl, pl.ds(pi * page, page)], sems.at[0, sl]))
            out.append(pltpu.make_async_copy(ckr_hbm.at[layer, pid], krbuf.at[sl, pi], sems.at[1, sl]))
        return out

    @pl.when(step == 0)
    def _():
        for cp in copies(b, c, slot):
            cp.start()

    @pl.when(step + 1 < n_steps)
    def _():
        nxt = step + 1
        for cp in copies(nxt // n_chunks, nxt % n_chunks, 1 - slot):
            cp.start()

    for cp in copies(b, c, slot):
        cp.wait()

    @pl.when(c == 0)
    def _():
        m_sc[...] = jnp.full(m_sc.shape, -jnp.inf, F32)
        l_sc[...] = jnp.zeros(l_sc.shape, F32)
        acc_sc[...] = jnp.zeros(acc_sc.shape, F32)

    qlat = qlat_ref[0]
    qpe = qpe_ref[0][:, :d_rope]
    kv16 = kvbuf[slot].astype(BF16)
    s_pe = jnp.concatenate([_dot(qpe, krbuf[slot, pi].astype(BF16)) for pi in range(pages_per_chunk)], axis=1)
    s = _dot_nt(qlat, kv16) + s_pe
    m_old = m_sc[...]
    m_new = jnp.maximum(m_old, jnp.max(s, axis=-1, keepdims=True))
    alpha = jnp.exp(m_old - m_new)
    p = jnp.exp(s - m_new)
    l_new = alpha * l_sc[...] + jnp.sum(p, axis=-1, keepdims=True)
    acc_new = alpha * acc_sc[...] + _dot(p.astype(BF16), kv16)
    m_sc[...] = m_new
    l_sc[...] = l_new
    acc_sc[...] = acc_new

    @pl.when(c == n_chunks - 1)
    def _():
        knl = knl_ref[0]
        s_new = (jnp.sum(qlat.astype(F32) * knl, axis=-1, keepdims=True)
                 + jnp.sum(qpe.astype(F32) * knp_ref[0], axis=-1, keepdims=True))
        m_fin = jnp.maximum(m_new, s_new)
        a_old = jnp.exp(m_new - m_fin)
        p_new = jnp.exp(s_new - m_fin)
        o_ref[0] = (a_old * acc_new + p_new * knl) / (a_old * l_new + p_new)


def sample_decode(page_table, qlat, qpe, k_new_lat, k_new_pe, cache_kv, cache_kr, layer, dims):
    ms, h, r_kv, d_rope = dims["MS"], dims["H"], dims["KV_LORA"], dims["QK_ROPE"]
    n_pages = page_table.shape[1]
    page = cache_kv.shape[2]
    ppc = min(DECODE_PAGES, n_pages)
    n_chunks = n_pages // ppc
    grid_spec = pltpu.PrefetchScalarGridSpec(
        num_scalar_prefetch=1,
        grid=(ms, n_chunks),
        in_specs=[pl.BlockSpec((1, h, r_kv), lambda b, c, pt: (b, 0, 0)),
                  pl.BlockSpec((1, h, LANES), lambda b, c, pt: (b, 0, 0)),
                  pl.BlockSpec((1, 1, r_kv), lambda b, c, pt: (b, 0, 0)),
                  pl.BlockSpec((1, 1, d_rope), lambda b, c, pt: (b, 0, 0)),
                  pl.BlockSpec(memory_space=pl.ANY),
                  pl.BlockSpec(memory_space=pl.ANY)],
        out_specs=pl.BlockSpec((1, h, r_kv), lambda b, c, pt: (b, 0, 0)),
        scratch_shapes=[pltpu.VMEM((2, ppc * page, r_kv), F32),
                        pltpu.VMEM((2, ppc, d_rope, page), F32),
                        pltpu.SemaphoreType.DMA((2, 2)),
                        pltpu.VMEM((h, 1), F32), pltpu.VMEM((h, 1), F32), pltpu.VMEM((h, r_kv), F32)])
    return pl.pallas_call(
        functools.partial(_decode_kernel, layer=layer, pages_per_chunk=ppc, n_chunks=n_chunks,
                          page=page, d_rope=d_rope),
        grid_spec=grid_spec,
        out_shape=jax.ShapeDtypeStruct((ms, h, r_kv), F32),
        compiler_params=_cparams("arbitrary", "arbitrary"),
        name="sample_decode",
    )(page_table, qlat, qpe, k_new_lat.reshape(ms, 1, r_kv), k_new_pe.reshape(ms, 1, d_rope), cache_kv, cache_kr)


def _sample_s5_kernel(u_ref, g_ref, h0r_ref, h0i_ref, ar_ref, ai_ref, bs_ref, cs_ref, d_ref, wglu_ref, bglu_ref,
                      br_ref, hr_ref, hi_ref):
    u = u_ref[...]
    u16 = u.astype(BF16)
    n_kt, kt, two_ns = bs_ref.shape
    ns = two_ns // 2
    ys = []
    for t in range(n_kt):
        bu = _dot(u16[:, t * kt:(t + 1) * kt], bs_ref[t])
        sl = slice(t * ns, (t + 1) * ns)
        ar, ai, h0r, h0i = ar_ref[:, sl], ai_ref[:, sl], h0r_ref[:, sl], h0i_ref[:, sl]
        hr = ar * h0r - ai * h0i + bu[:, :ns]
        hi = ar * h0i + ai * h0r + bu[:, ns:]
        hr_ref[:, sl] = hr
        hi_ref[:, sl] = hi
        hcat = jnp.concatenate([hr, hi], axis=1).astype(BF16)
        ys.append(_dot(hcat, cs_ref[t]))
    y = jnp.concatenate(ys, axis=1) + d_ref[...] * u
    z = _gelu_tanh(y)
    o = z * _sigmoid(_dot(z.astype(BF16), wglu_ref[...]) + bglu_ref[...])
    br_ref[...] = (o * _silu(g_ref[...])).astype(BF16)


def sample_s5(proj, h0r, h0i, lp, dims):
    ms, w, off = dims["MS"], dims["W"], dims["off"]
    n_state = h0r.shape[1]
    tr = min(64, ms)
    n_kt, kt, two_ns = lp["s5_bs"].shape
    row = lambda width, cb=0: pl.BlockSpec((tr, width), lambda i: (i, cb))
    full = lambda *shape: pl.BlockSpec(shape, lambda i: (0,) * len(shape))
    return pl.pallas_call(
        _sample_s5_kernel,
        grid=(ms // tr,),
        in_specs=[row(w, off["u_s5"] // w), row(w, off["g_s5"] // w), row(n_state), row(n_state),
                  full(1, n_state), full(1, n_state), full(n_kt, kt, two_ns), full(n_kt, two_ns, kt),
                  full(1, w), full(w, w), full(1, w)],
        out_specs=[row(w), row(n_state), row(n_state)],
        out_shape=[jax.ShapeDtypeStruct((ms, w), BF16), jax.ShapeDtypeStruct((ms, n_state), F32),
                   jax.ShapeDtypeStruct((ms, n_state), F32)],
        compiler_params=_cparams("parallel"),
        name="sample_s5",
    )(proj, proj, h0r, h0i, lp["s5_a_re"], lp["s5_a_im"], lp["s5_bs"], lp["s5_cs"], lp["s5_d"],
      lp["w_glu"], lp["b_glu"])


def _sample_mix_kernel(olat_ref, wuv_ref, gmla_ref, up_ref, gp_ref, past_ref, pw_ref, psc_ref,
                       ug_ref, vg_ref, gg_ref, lng_ref, lnb_ref, ws0_ref, bs0_ref,
                       bmla_ref, bpool_ref, bgm_ref, vrow_ref, *, n_heads, d_v, pool_cnt):
    gm = gmla_ref[...]
    for h in range(n_heads):
        o = _dot(olat_ref[h].astype(BF16), wuv_ref[h])
        bmla_ref[:, h * d_v:(h + 1) * d_v] = (o * _silu(gm[:, h * d_v:(h + 1) * d_v])).astype(BF16)

    u = up_ref[...]
    n_past = past_ref.shape[0]
    gw = u.shape[1] // len(POOL_WINDOWS)
    outs = []
    for gi, win in enumerate(POOL_WINDOWS):
        sl = slice(gi * gw, (gi + 1) * gw)
        s = u[:, sl]
        for k in range(1, win):
            s = s + past_ref[n_past - k][:, sl]
        dlt = s / pool_cnt[gi] - u[:, sl]
        outs.append(_dot(dlt.astype(BF16), pw_ref[gi]))
    yp = jnp.concatenate(outs, axis=1) * psc_ref[...]
    bpool_ref[...] = (yp * _silu(gp_ref[...])).astype(BF16)

    v = vg_ref[...]
    mu = jnp.mean(v, axis=-1, keepdims=True)
    vc = v - mu
    vn = vc * lax.rsqrt(jnp.mean(vc * vc, axis=-1, keepdims=True) + NORM_EPS) * lng_ref[...] + lnb_ref[...]
    vrow_ref[...] = vn
    mixed = ws0_ref[...] * vn + bs0_ref[...]
    bgm_ref[...] = (ug_ref[...] * mixed * _silu(gg_ref[...])).astype(BF16)


def sample_mix(proj, olat_hm, past_t, lp, dims):
    ms, w, h, d_v, r_kv, off = dims["MS"], dims["W"], dims["H"], dims["V_HEAD"], dims["KV_LORA"], dims["off"]
    tr = min(64, ms)
    n_past = past_t.shape[0]
    n_g = len(POOL_WINDOWS)
    pool_cnt = tuple(float(min(dims["N_PAST"] + 1, win)) for win in POOL_WINDOWS)
    row = lambda key: pl.BlockSpec((tr, w), lambda i: (i, off[key] // w))
    full = lambda *shape: pl.BlockSpec(shape, lambda i: (0,) * len(shape))
    out_row = pl.BlockSpec((tr, w), lambda i: (i, 0))
    return pl.pallas_call(
        functools.partial(_sample_mix_kernel, n_heads=h, d_v=d_v, pool_cnt=pool_cnt),
        grid=(ms // tr,),
        in_specs=[pl.BlockSpec((h, tr, r_kv), lambda i: (0, i, 0)), full(h, r_kv, d_v), row("g_mla"),
                  row("u_pool"), row("g_pool"), pl.BlockSpec((n_past, tr, w), lambda i: (0, i, 0)),
                  full(n_g, w // n_g, w // n_g), full(1, w),
                  row("u_gm"), row("v_gm"), row("g_gm"), full(1, w), full(1, w), full(1, w), full(1, w)],
        out_specs=[out_row, out_row, out_row, out_row],
        out_shape=[jax.ShapeDtypeStruct((ms, w), BF16), jax.ShapeDtypeStruct((ms, w), BF16),
                   jax.ShapeDtypeStruct((ms, w), BF16), jax.ShapeDtypeStruct((ms, w), F32)],
        compiler_params=_cparams("parallel"),
        name="sample_mix",
    )(olat_hm, lp["w_uv_h"], proj, proj, proj, past_t, lp["pool_w"], lp["pool_scale"],
      proj, proj, proj, lp["ln_g"], lp["ln_b"], lp["gmlp_ws0"], lp["gmlp_bs0"])


def _s5_params(lam_re, lam_im, log_dt, b_re, b_im, c_re, c_im, d):
    hp = lax.Precision.HIGHEST
    g, p, gc = b_re.shape
    t = S5_CHUNK
    dt = jnp.exp(log_dt)[:, None]
    ld_re, ld_im = lam_re * dt, lam_im * dt
    mag = jnp.exp(ld_re)
    a_re, a_im = mag * jnp.cos(ld_im), mag * jnp.sin(ld_im)
    den = lam_re * lam_re + lam_im * lam_im
    num_re, num_im = a_re - 1.0, a_im
    coef_re = (num_re * lam_re + num_im * lam_im) / den
    coef_im = (num_im * lam_re - num_re * lam_im) / den
    bb_re = coef_re[..., None] * b_re - coef_im[..., None] * b_im
    bb_im = coef_re[..., None] * b_im + coef_im[..., None] * b_re

    pw_re, pw_im = [jnp.ones_like(a_re)], [jnp.zeros_like(a_im)]
    for _ in range(t):
        r, i = pw_re[-1], pw_im[-1]
        pw_re.append(r * a_re - i * a_im)
        pw_im.append(r * a_im + i * a_re)
    pw_re, pw_im = jnp.stack(pw_re), jnp.stack(pw_im)

    e_re = pw_re[:t, ..., None] * bb_re - pw_im[:t, ..., None] * bb_im
    e_im = pw_re[:t, ..., None] * bb_im + pw_im[:t, ..., None] * bb_re

    cb = (jnp.einsum("gcp,kgpd->kgdc", c_re, e_re, precision=hp)
          - jnp.einsum("gcp,kgpd->kgdc", c_im, e_im, precision=hp))
    cb = cb.at[0].add(jnp.eye(gc, dtype=F32)[None] * d[:, None, :])

    kt = LANES
    gpt = kt // gc
    n_kt = g // gpt
    eye = jnp.eye(gpt, dtype=F32)
    tile_rows = jnp.arange(kt)[:, None] // gc
    tile_cols = jnp.arange(kt)[None, :] // gc
    bd = jnp.where(tile_rows == tile_cols, jnp.tile(cb.reshape(t, n_kt, kt, gc), (1, 1, 1, gpt)), 0.0)
    bd = bd.transpose(1, 0, 2, 3)

    def packed(x_re, x_im):
        r = x_re.reshape(t + 1, n_kt, gpt * p).transpose(1, 0, 2)
        i = x_im.reshape(t + 1, n_kt, gpt * p).transpose(1, 0, 2)
        return jnp.concatenate([r, r], axis=2), jnp.concatenate([-i, i], axis=2)

    pr, pi = packed(pw_re, pw_im)

    def bdiag_in(bb):
        x = bb.reshape(n_kt, gpt, p, gc)
        return jnp.einsum("tgpc,gh->tgchp", x, eye).reshape(n_kt, kt, gpt * p)

    def bdiag_out(cc):
        x = cc.reshape(n_kt, gpt, gc, p)
        return jnp.einsum("tgcp,gh->tgphc", x, eye).reshape(n_kt, gpt * p, kt)

    bs = jnp.concatenate([bdiag_in(bb_re), bdiag_in(bb_im)], axis=2)
    cs = jnp.concatenate([bdiag_out(c_re), -bdiag_out(c_im)], axis=1)
    return {"s5_bd": bd.astype(BF16), "s5_pr": pr, "s5_pi": pi,
            "s5_a_re": a_re.reshape(1, g * p), "s5_a_im": a_im.reshape(1, g * p),
            "s5_bs": bs.astype(BF16), "s5_cs": cs.astype(BF16), "s5_d": d.reshape(1, g * gc)}


def _rope_tables(pos, d_rope):
    half = d_rope // 2
    freqs = ROPE_THETA ** (-jnp.arange(half, dtype=F32) / half)
    ang = pos.astype(F32)[:, None] * freqs[None, :]
    cos, sin = jnp.cos(ang), jnp.sin(ang)
    z = jnp.zeros((pos.shape[0], LANES - d_rope), F32)
    zh = jnp.zeros_like(cos)
    return (jnp.concatenate([cos, cos, z], axis=1),
            jnp.concatenate([-sin, zh, z], axis=1),
            jnp.concatenate([zh, sin, z], axis=1))


def _layer_params(l, dims, w_in, norm_attn, mla_q_norm, mla_kv_norm, mla_w_uq, mla_w_uk, mla_w_uv,
                  s5, s5_w_glu, s5_b_glu, pool_w, pool_scale, gmlp_ln_g, gmlp_ln_b, gmlp_w_s, gmlp_b_s,
                  w_branch, w_out):
    r_q, r_kv, d_rope, d_nope, h, w = (dims["Q_LORA"], dims["KV_LORA"], dims["QK_ROPE"], dims["QK_NOPE"],
                                       dims["H"], dims["W"])
    lp = {"norm": norm_attn[l][None, :],
          "q_norm": mla_q_norm[l][None, :], "kv_norm": mla_kv_norm[l][None, :]}
    lp["w_in"], lp["w_kpe"] = pack_w_in(w_in, l, r_q + r_kv, d_rope, INPROJ_COLS)
    wuq = mla_w_uq[l]
    lp["w_uq_pad"] = jnp.pad(wuq, ((0, 0), (0, 0), (0, LANES - d_rope))).reshape(r_q, -1).astype(BF16)
    lp["w_uk_flat"] = mla_w_uk[l].reshape(r_kv, -1).astype(BF16)
    lp["w_uv_flat"] = mla_w_uv[l].reshape(r_kv, -1).astype(BF16)
    lp["w_uk_t"] = mla_w_uk[l].transpose(1, 2, 0).astype(BF16)
    lp["w_uv_h"] = mla_w_uv[l].transpose(1, 0, 2).astype(BF16)
    lp.update(s5)
    lp["w_glu"] = s5_w_glu[l].astype(BF16)
    lp["b_glu"] = s5_b_glu[l][None, :]
    lp["pool_w"] = pool_w[l].astype(BF16)
    lp["pool_scale"] = pool_scale[l][None, :]
    lp["ln_g"] = gmlp_ln_g[l][None, :]
    lp["ln_b"] = gmlp_ln_b[l][None, :]
    chunk = gmlp_w_s.shape[2]
    ws = gmlp_w_s[l] * jnp.tril(jnp.ones((chunk, chunk), F32))[None]
    n_g = ws.shape[0]
    lp["gmlp_ws"] = ws.astype(BF16)
    lp["gmlp_bs_t"] = gmlp_b_s[l].T
    lp["gmlp_ws0"] = jnp.repeat(ws[:, 0, 0], w // n_g)[None, :]
    lp["gmlp_bs0"] = jnp.repeat(gmlp_b_s[l][:, 0], w // n_g)[None, :]
    lp["w_branch"] = w_branch[l].astype(BF16)
    lp["w_out"] = w_out[l].astype(BF16)
    return lp


def kernel(x_prompt, x_sample, cache_kv_latent, cache_k_rope, state_s5_re, state_s5_im, state_pool, page_table,
           norm_attn, w_in, mla_q_norm, mla_kv_norm, mla_w_uq, mla_w_uk, mla_w_uv, s5_lambda_re, s5_lambda_im,
           s5_log_dt, s5_b_re, s5_b_im, s5_c_re, s5_c_im, s5_d, s5_w_glu, s5_b_glu, pool_w, pool_scale,
           gmlp_ln_g, gmlp_ln_b, gmlp_w_s, gmlp_b_s, w_branch, w_out, norm_final):
    b, l, d = x_prompt.shape
    ms, dec_seq, _ = x_sample.shape
    assert dec_seq == 1, "one new token per sampled sequence"
    depth = w_in.shape[0]
    w = w_branch.shape[2]
    n_branch = w_branch.shape[1]
    r_q, r_kv = mla_q_norm.shape[1], mla_kv_norm.shape[1]
    h, d_nope, d_v = mla_w_uk.shape[2], mla_w_uk.shape[3], mla_w_uv.shape[3]
    d_rope = mla_w_uq.shape[3] - d_nope
    n_past = page_table.shape[1] * cache_kv_latent.shape[2]
    pool_buf = state_pool.shape[2]
    assert d_nope == LANES and d_v == LANES and d_rope <= LANES and pool_buf == max(POOL_WINDOWS) - 1

    names = ("c_q", "c_kv", "g_mla", "u_s5", "g_s5", "u_pool", "g_pool", "u_gm", "v_gm", "g_gm", "g_merge")
    widths = (r_q, r_kv, w, w, w, w, w, w, w, w, n_branch * d)
    off, acc = {}, 0
    for nm, wd in zip(names, widths):
        off[nm] = acc
        acc += wd
    dims = {"B": b, "L": l, "D": d, "MS": ms, "W": w, "H": h, "QK_NOPE": d_nope, "QK_ROPE": d_rope,
            "V_HEAD": d_v, "Q_LORA": r_q, "KV_LORA": r_kv, "S5_G": s5_b_re.shape[1], "S5_P": s5_b_re.shape[2],
            "S5_GC": s5_b_re.shape[3], "CHUNK": gmlp_w_s.shape[2], "N_PAST": n_past, "off": off,
            "scale": float((d_nope + d_rope) ** -0.5)}

    rope_p = _rope_tables(jnp.arange(l, dtype=jnp.int32), d_rope)
    rope_s = _rope_tables(jnp.full((1,), n_past, dtype=jnp.int32), d_rope)
    final_gain = norm_final[None, :]
    cache_kr_t = cache_k_rope.transpose(0, 1, 3, 2)
    w_in_t = w_in.transpose(0, 2, 1)

    mp = b * l
    hp = x_prompt.reshape(mp, d)
    hs = x_sample.reshape(ms, d)
    tm_p = min(INPROJ_TILE, l)
    tn_in = INPROJ_COLS
    new_p, new_s = [], []
    for layer in range(depth):
        s5 = _s5_params(s5_lambda_re[layer], s5_lambda_im[layer], s5_log_dt[layer], s5_b_re[layer],
                        s5_b_im[layer], s5_c_re[layer], s5_c_im[layer], s5_d[layer])
        lp = _layer_params(layer, dims, w_in_t, norm_attn, mla_q_norm, mla_kv_norm, mla_w_uq, mla_w_uk, mla_w_uv,
                           s5, s5_w_glu, s5_b_glu, pool_w, pool_scale, gmlp_ln_g, gmlp_ln_b, gmlp_w_s, gmlp_b_s,
                           w_branch, w_out)
        last = layer == depth - 1

        proj, gates, xn = norm_inproj(hp, lp["norm"], lp["w_in"], off["g_merge"], tm_p, tn_in)
        br_mla, klat, kpe = prompt_mla(proj, xn, lp, dims, rope_p)
        br_s5, s5r, s5i = prompt_s5(proj, lp, dims)
        br_pool = prompt_pool(proj, lp, dims)
        br_gm = prompt_gmlp(proj, lp, dims)
        merged = merge_branches([br_mla, br_s5, br_pool, br_gm], gates, lp["w_branch"], tm_p, 512)
        pool_tail = proj.reshape(b, l, -1)[:, l - pool_buf:, off["u_pool"]:off["u_pool"] + w]
        new_p.append((klat.reshape(b, l, r_kv), kpe.reshape(b, l, d_rope), s5r, s5i, pool_tail))
        hp = out_proj(hp, merged, lp["w_out"], min(ROW_TILE, l), final_gain if last else None)

        proj_s, gates_s, xn_s = norm_inproj(hs, lp["norm"], lp["w_in"], off["g_merge"], ms, tn_in)
        klat_s, kpe_s, qlat_hm, qpe_hm = sample_qkv(proj_s, xn_s, lp, dims, rope_s)
        olat = sample_decode(page_table, qlat_hm.transpose(1, 0, 2), qpe_hm.transpose(1, 0, 2), klat_s, kpe_s,
                             cache_kv_latent, cache_kr_t, layer, dims)
        g_s, p_s = state_s5_re.shape[2], state_s5_re.shape[3]
        br_s5_s, s5r_s, s5i_s = sample_s5(proj_s, state_s5_re[layer].reshape(ms, g_s * p_s),
                                          state_s5_im[layer].reshape(ms, g_s * p_s), lp, dims)
        past = state_pool[layer]
        br_mla_s, br_pool_s, br_gm_s, v_rows = sample_mix(proj_s, olat.transpose(1, 0, 2),
                                                          past.transpose(1, 0, 2), lp, dims)
        merged_s = merge_branches([br_mla_s, br_s5_s, br_pool_s, br_gm_s], gates_s, lp["w_branch"], ms, 512)
        u_pool_s = proj_s[:, off["u_pool"]:off["u_pool"] + w]
        new_s.append((klat_s.reshape(ms, 1, r_kv), kpe_s.reshape(ms, 1, d_rope),
                      s5r_s.reshape(ms, g_s, p_s), s5i_s.reshape(ms, g_s, p_s),
                      jnp.concatenate([past[:, 1:], u_pool_s[:, None, :]], axis=1),
                      v_rows.reshape(ms, 1, w)))
        hs = out_proj(hs, merged_s, lp["w_out"], ms, final_gain if last else None)

    stack = lambda states, i: jnp.stack([s[i] for s in states], axis=0)
    return (hp.reshape(b, l, d), hs.reshape(ms, 1, d),
            stack(new_p, 0), stack(new_p, 1), stack(new_p, 2), stack(new_p, 3), stack(new_p, 4),
            stack(new_s, 0), stack(new_s, 1), stack(new_s, 2), stack(new_s, 3), stack(new_s, 4),
            stack(new_s, 5))
```

```python
import functools
import math

import jax
import jax.numpy as jnp
from jax import lax
from jax.experimental import pallas as pl
from jax.experimental.pallas import tpu as pltpu

F32 = jnp.float32
BF16 = jnp.bfloat16

NORM_EPS = 1e-6
ROPE_THETA = 10000.0
POOL_WINDOWS = (2, 4, 8, 16)
S5_CHUNK = 16
LANES = 128
VMEM_LIMIT = 56 * 1024 * 1024
GELU_C = math.sqrt(2.0 / math.pi)
ROW_TILE = 512
INPROJ_TILE = 1024
INPROJ_COLS = 1024
ATTN_TILE = 1024
ATTN_CHUNK = 512
LOG2_E = math.log2(math.e)
DECODE_PAGES = 64
GATE_SUB = 256
PACK_ROWS = 1024


def _cparams(*sem):
    return pltpu.CompilerParams(dimension_semantics=sem, vmem_limit_bytes=VMEM_LIMIT)


def _rms(x, g):
    return x * lax.rsqrt(jnp.mean(x * x, axis=-1, keepdims=True) + NORM_EPS) * g


def _sigmoid(x):
    return 1.0 / (1.0 + jnp.exp(-x))


def _silu(x):
    return x * _sigmoid(x)


def _gelu_tanh(x):
    return x * (0.5 * (1.0 + jnp.tanh(GELU_C * (x + 0.044715 * (x * x * x)))))


def _rope128(x, c, s1, s2):
    return x * c + pltpu.roll(x, 96, axis=1) * s1 + pltpu.roll(x, 32, axis=1) * s2


def _dot(a, b):
    return jnp.dot(a, b, preferred_element_type=F32)


def _dot_nt(a, b):
    return lax.dot_general(a, b, (((1,), (1,)), ((), ())), preferred_element_type=F32)


def _norm_inproj_kernel(x_ref, g_ref, w_ref, proj_ref, gate_ref, xn_ref, *, n_proj_tiles):
    j = pl.program_id(1)

    @pl.when(j == 0)
    def _():
        xn_ref[...] = _rms(x_ref[...], g_ref[...]).astype(BF16)

    @pl.when(j < n_proj_tiles)
    def _():
        proj_ref[...] = _dot(xn_ref[...], w_ref[...])

    @pl.when(j >= n_proj_tiles)
    def _():
        tn = w_ref.shape[1]
        for c0 in range(0, tn, GATE_SUB):
            cols = slice(c0, min(c0 + GATE_SUB, tn))
            gate_ref[:, cols] = _sigmoid(_dot(xn_ref[...], w_ref[:, cols])).astype(BF16)


def norm_inproj(x, g, w, n_proj, tm, tn):
    m, d = x.shape
    n = w.shape[1]
    npt = n_proj // tn
    return pl.pallas_call(
        functools.partial(_norm_inproj_kernel, n_proj_tiles=npt),
        grid=(m // tm, n // tn),
        in_specs=[pl.BlockSpec((tm, d), lambda i, j: (i, 0)),
                  pl.BlockSpec((1, d), lambda i, j: (0, 0)),
                  pl.BlockSpec((d, tn), lambda i, j: (0, j))],
        out_specs=[pl.BlockSpec((tm, tn), lambda i, j: (i, jnp.minimum(j, npt - 1))),
                   pl.BlockSpec((tm, tn), lambda i, j: (i, jnp.maximum(j - npt, 0))),
                   pl.BlockSpec((tm, d), lambda i, j: (i, 0))],
        out_shape=[jax.ShapeDtypeStruct((m, n_proj), F32), jax.ShapeDtypeStruct((m, n - n_proj), BF16),
                   jax.ShapeDtypeStruct((m, d), BF16)],
        compiler_params=_cparams("parallel", "arbitrary"),
        name="norm_inproj",
    )(x, g, w)


def _pack_w_in_kernel(a_ref, o_ref):
    o_ref[...] = a_ref[0].T.astype(BF16)


def _pack_w_hole_kernel(a_ref, o_ref):
    a = a_ref[0]
    pad = jnp.zeros((o_ref.shape[1] - a.shape[0], a.shape[1]), F32)
    o_ref[...] = jnp.concatenate([a, pad], axis=0).T.astype(BF16)


def pack_w_in(w_in_t, layer, hole_start, hole, tn):
    _, n, k = w_in_t.shape
    assert hole_start == tn and hole_start % hole == 0 and hole < LANES
    n_out = n - hole
    tk = min(PACK_ROWS, k)
    main = pl.pallas_call(
        _pack_w_in_kernel,
        grid=(n_out // tn, k // tk),
        in_specs=[pl.BlockSpec((pl.Element(1), pl.Element(tn), pl.Element(tk)),
                               lambda j, r: (layer, jnp.where(j == 0, 0, hole // 8 + j * (tn // 8)) * 8, r * tk))],
        out_specs=pl.BlockSpec((tk, tn), lambda j, r: (r, j)),
        out_shape=jax.ShapeDtypeStruct((k, n_out), BF16),
        compiler_params=_cparams("parallel", "parallel"),
        name="pack_w_in",
    )(w_in_t)
    hole_cols = pl.pallas_call(
        _pack_w_hole_kernel,
        grid=(1,),
        in_specs=[pl.BlockSpec((1, hole, k), lambda i: (layer, hole_start // hole, 0))],
        out_specs=pl.BlockSpec((k, LANES), lambda i: (0, 0)),
        out_shape=jax.ShapeDtypeStruct((k, LANES), BF16),
        compiler_params=_cparams("arbitrary"),
        name="pack_w_hole",
    )(w_in_t)
    return main, hole_cols


def _merge_kernel(b0_ref, b1_ref, b2_ref, b3_ref, g0_ref, g1_ref, g2_ref, g3_ref, w_ref, o_ref):
    acc = None
    for k, (b_ref, g_ref) in enumerate(((b0_ref, g0_ref), (b1_ref, g1_ref),
                                        (b2_ref, g2_ref), (b3_ref, g3_ref))):
        t = g_ref[...].astype(F32) * _dot(b_ref[...], w_ref[k])
        acc = t if acc is None else acc + t
    o_ref[...] = acc.astype(BF16)


def merge_branches(branches, gates, w_branch, tm, tn):
    m, w = branches[0].shape
    d = w_branch.shape[2]
    nb = len(branches)
    gate_specs = [pl.BlockSpec((tm, tn), functools.partial(
        lambda i, j, k: (i, k * d // tn + j), k=k)) for k in range(nb)]
    return pl.pallas_call(
        _merge_kernel,
        grid=(m // tm, d // tn),
        in_specs=[pl.BlockSpec((tm, w), lambda i, j: (i, 0))] * nb + gate_specs
        + [pl.BlockSpec((nb, w, tn), lambda i, j: (0, 0, j))],
        out_specs=pl.BlockSpec((tm, tn), lambda i, j: (i, j)),
        out_shape=jax.ShapeDtypeStruct((m, d), BF16),
        compiler_params=_cparams("parallel", "arbitrary"),
        name="merge_branches",
    )(*branches, *([gates] * nb), w_branch)


def _outproj_kernel(h_ref, x_ref, w_ref, o_ref):
    o_ref[...] = h_ref[...] + _dot(x_ref[...], w_ref[...])


def _outproj_norm_kernel(h_ref, x_ref, w_ref, g_ref, o_ref):
    o_ref[...] = _rms(h_ref[...] + _dot(x_ref[...], w_ref[...]), g_ref[...])


def out_proj(h, merged, w_out, tm, final_gain=None):
    m, d = h.shape
    specs = [pl.BlockSpec((tm, d), lambda i: (i, 0)),
             pl.BlockSpec((tm, d), lambda i: (i, 0)),
             pl.BlockSpec((d, d), lambda i: (0, 0))]
    args = [h, merged, w_out]
    body = _outproj_kernel
    if final_gain is not None:
        specs.append(pl.BlockSpec((1, d), lambda i: (0, 0)))
        args.append(final_gain)
        body = _outproj_norm_kernel
    return pl.pallas_call(
        body,
        grid=(m // tm,),
        in_specs=specs,
        out_specs=pl.BlockSpec((tm, d), lambda i: (i, 0)),
        out_shape=jax.ShapeDtypeStruct((m, d), F32),
        compiler_params=_cparams("parallel"),
        name="out_proj",
    )(*args)


def _kv_side_kernel(ckv_ref, xn_ref, wkpe_ref, g_ref, wuk_ref, wuv_ref, c_ref, s1_ref, s2_ref,
                    klat_ref, kpe_ref, kp_ref, v_ref, *, n_heads, d_nope, d_v, d_rope):
    klat = _rms(ckv_ref[...], g_ref[...])
    klat_ref[...] = klat
    kl16 = klat.astype(BF16)
    kpe = _rope128(_dot(xn_ref[...], wkpe_ref[...]), c_ref[...], s1_ref[...], s2_ref[...])
    kpe_ref[...] = kpe[:, :d_rope]
    kpe16 = kpe.astype(BF16)
    knope = _dot(kl16, wuk_ref[...])
    v = _dot(kl16, wuv_ref[...])
    for h in range(n_heads):
        kp_ref[0, h, :, 0:d_nope] = knope[:, h * d_nope:(h + 1) * d_nope].astype(BF16)
        kp_ref[0, h, :, d_nope:d_nope + LANES] = kpe16
        v_ref[0, h] = v[:, h * d_v:(h + 1) * d_v].astype(BF16)


def _q_side_kernel(cq_ref, g_ref, wuq_ref, c_ref, s1_ref, s2_ref, qp_ref, *, n_heads, d_nope, scale):
    cq = _rms(cq_ref[...], g_ref[...]).astype(BF16)
    q = _dot(cq, wuq_ref[...])
    c, s1, s2 = c_ref[...], s1_ref[...], s2_ref[...]
    dq = d_nope + LANES
    for h in range(n_heads):
        qp_ref[0, h, :, 0:d_nope] = (q[:, h * dq:h * dq + d_nope] * scale).astype(BF16)
        pe = _rope128(q[:, h * dq + d_nope:(h + 1) * dq], c, s1, s2)
        qp_ref[0, h, :, d_nope:dq] = (pe * scale).astype(BF16)


def _flash_kernel(q_ref, k_ref, v_ref, g_ref, o_ref, m_sc, l_sc, acc_sc, s_sc, *, tc):
    i = pl.program_id(2)
    tq = q_ref.shape[2]
    per_tile = tq // tc
    assert per_tile % 2 == 0
    m_sc[...] = jnp.full(m_sc.shape, -jnp.inf, F32)
    l_sc[...] = jnp.zeros(l_sc.shape, F32)
    acc_sc[...] = jnp.zeros(acc_sc.shape, F32)

    def keys(ref, c):
        return ref[0, 0, pl.ds(pl.multiple_of(c * tc, tc), tc), :]

    def accumulate(s, c, row0, n_rows):
        rows = pl.ds(row0, n_rows)
        m_old = m_sc[rows, :]
        m_new = jnp.maximum(m_old, jnp.max(s, axis=-1, keepdims=True))
        alpha = jnp.exp2(m_old - m_new)
        p = jnp.exp2(s - m_new)
        l_sc[rows, :] = alpha * l_sc[rows, :] + jnp.sum(p, axis=-1, keepdims=True)
        acc_sc[rows, :] = alpha * acc_sc[rows, :] + _dot(p.astype(BF16), keys(v_ref, c))
        m_sc[rows, :] = m_new

    n_full = i * per_tile
    s_sc[0] = _dot_nt(q_ref[0, 0], keys(k_ref, 0))

    def body(t, carry):
        c = 2 * t
        s_sc[1] = _dot_nt(q_ref[0, 0], keys(k_ref, c + 1))
        accumulate(s_sc[0], c, 0, tq)
        s_sc[0] = _dot_nt(q_ref[0, 0], keys(k_ref, c + 2))
        accumulate(s_sc[1], c + 1, 0, tq)
        return carry

    lax.fori_loop(0, n_full // 2, body, 0)

    for d in range(per_tile):
        c = n_full + d
        n_rows = tq - d * tc
        if d == 0:
            s = s_sc[0]
        else:
            s = _dot_nt(q_ref[0, 0, pl.ds(d * tc, n_rows), :], keys(k_ref, c))
        row = lax.broadcasted_iota(jnp.int32, s.shape, 0)
        col = lax.broadcasted_iota(jnp.int32, s.shape, 1)
        accumulate(jnp.where(col <= row, s, -jnp.inf), c, d * tc, n_rows)

    o_ref[...] = (acc_sc[...] / l_sc[...] * _silu(g_ref[...])).astype(BF16)


def prompt_mla(proj, xn, lp, dims, rope_tabs):
    b, l, h = dims["B"], dims["L"], dims["H"]
    d_nope, d_v, d_rope = dims["QK_NOPE"], dims["V_HEAD"], dims["QK_ROPE"]
    r_q, r_kv, d = dims["Q_LORA"], dims["KV_LORA"], dims["D"]
    off = dims["off"]
    m = b * l
    tm = min(ROW_TILE, l)
    nl = l // tm
    dk = d_nope + LANES
    c_tab, s1_tab, s2_tab = rope_tabs
    tab_spec = pl.BlockSpec((tm, LANES), lambda i: (i % nl, 0))

    klat, kpe, kp, v = pl.pallas_call(
        functools.partial(_kv_side_kernel, n_heads=h, d_nope=d_nope, d_v=d_v, d_rope=d_rope),
        grid=(m // tm,),
        in_specs=[pl.BlockSpec((tm, r_kv), lambda i: (i, off["c_kv"] // r_kv)),
                  pl.BlockSpec((tm, d), lambda i: (i, 0)),
                  pl.BlockSpec((d, LANES), lambda i: (0, 0)),
                  pl.BlockSpec((1, r_kv), lambda i: (0, 0)),
                  pl.BlockSpec((r_kv, h * d_nope), lambda i: (0, 0)),
                  pl.BlockSpec((r_kv, h * d_v), lambda i: (0, 0)),
                  tab_spec, tab_spec, tab_spec],
        out_specs=[pl.BlockSpec((tm, r_kv), lambda i: (i, 0)),
                   pl.BlockSpec((tm, d_rope), lambda i: (i, 0)),
                   pl.BlockSpec((1, h, tm, dk), lambda i: (i // nl, 0, i % nl, 0)),
                   pl.BlockSpec((1, h, tm, d_v), lambda i: (i // nl, 0, i % nl, 0))],
        out_shape=[jax.ShapeDtypeStruct((m, r_kv), F32),
                   jax.ShapeDtypeStruct((m, d_rope), F32),
                   jax.ShapeDtypeStruct((b, h, l, dk), BF16),
                   jax.ShapeDtypeStruct((b, h, l, d_v), BF16)],
        compiler_params=_cparams("parallel"),
        name="prompt_kv_side",
    )(proj, xn, lp["w_kpe"], lp["kv_norm"], lp["w_uk_flat"], lp["w_uv_flat"], c_tab, s1_tab, s2_tab)

    qp = pl.pallas_call(
        functools.partial(_q_side_kernel, n_heads=h, d_nope=d_nope, scale=dims["scale"] * LOG2_E),
        grid=(m // tm,),
        in_specs=[pl.BlockSpec((tm, r_q), lambda i: (i, off["c_q"] // r_q)),
                  pl.BlockSpec((1, r_q), lambda i: (0, 0)),
                  pl.BlockSpec((r_q, h * dk), lambda i: (0, 0)),
                  tab_spec, tab_spec, tab_spec],
        out_specs=pl.BlockSpec((1, h, tm, dk), lambda i: (i // nl, 0, i % nl, 0)),
        out_shape=jax.ShapeDtypeStruct((b, h, l, dk), BF16),
        compiler_params=_cparams("parallel"),
        name="prompt_q_side",
    )(proj, lp["q_norm"], lp["w_uq_pad"], c_tab, s1_tab, s2_tab)

    tq = min(ATTN_TILE, l)
    nq = l // tq
    tc = min(ATTN_CHUNK, tq // 2)
    br = pl.pallas_call(
        functools.partial(_flash_kernel, tc=tc),
        grid=(b, h, nq),
        in_specs=[pl.BlockSpec((1, 1, tq, dk), lambda bi, hi, i: (bi, hi, i, 0)),
                  pl.BlockSpec((1, 1, l, dk), lambda bi, hi, i: (bi, hi, 0, 0)),
                  pl.BlockSpec((1, 1, l, d_v), lambda bi, hi, i: (bi, hi, 0, 0)),
                  pl.BlockSpec((tq, d_v), lambda bi, hi, i: (bi * nq + i, off["g_mla"] // d_v + hi))],
        out_specs=pl.BlockSpec((tq, d_v), lambda bi, hi, i: (bi * nq + i, hi)),
        out_shape=jax.ShapeDtypeStruct((m, h * d_v), BF16),
        scratch_shapes=[pltpu.VMEM((tq, 1), F32), pltpu.VMEM((tq, 1), F32), pltpu.VMEM((tq, d_v), F32),
                        pltpu.VMEM((2, tq, tc), F32)],
        compiler_params=_cparams("parallel", "parallel", "parallel"),
        name="prompt_flash",
    )(qp, kp, v, proj)
    return br, klat, kpe


def _cmul(x, pr, pi):
    half = x.shape[1] // 2
    swapped = jnp.concatenate([x[:, half:], x[:, :half]], axis=1)
    return x * pr + swapped * pi


def _s5_chunk_kernel(u_ref, bdrev_ref, smat_ref, omat_ref, pr_ref, pi_ref, y_ref, hfin_ref, ucat_sc, *, t_chunk):
    t = t_chunk
    n = u_ref.shape[0] // t
    kt = u_ref.shape[1]
    for i in range(t):
        ucat_sc[:, i * kt:(i + 1) * kt] = u_ref[pl.ds(i, n, stride=t), :].astype(BF16)
    x = _dot(ucat_sc[...], smat_ref[0])

    nidx = lax.broadcasted_iota(jnp.int32, (n, 1), 0)
    ar, ai = pr_ref[0], pi_ref[0]
    d = 1
    while d < n:
        x = x + _cmul(jnp.where(nidx >= d, pltpu.roll(x, d, axis=0), 0.0), ar, ai)
        ar, ai = ar * ar - ai * ai, 2.0 * ar * ai
        d *= 2
    hfin_ref[0, 0] = x[n - 1:n, :]
    h_prev = jnp.where(nidx >= 1, pltpu.roll(x, 1, axis=0), 0.0)

    y_inter = _dot(h_prev.astype(BF16), omat_ref[0])
    for j in range(t):
        intra = _dot(ucat_sc[:, 0:(j + 1) * kt], bdrev_ref[0, (t - 1 - j) * kt:, :])
        y_ref[pl.ds(j, n, stride=t), :] = y_inter[:, j * kt:(j + 1) * kt] + intra


def _glu_gate_kernel(y_ref, w_ref, b_ref, g_ref, o_ref):
    z = _gelu_tanh(y_ref[...])
    o = z * _sigmoid(_dot(z.astype(BF16), w_ref[...]) + b_ref[...])
    o_ref[...] = (o * _silu(g_ref[...])).astype(BF16)


def prompt_s5(proj, lp, dims):
    b, l, w, off = dims["B"], dims["L"], dims["W"], dims["off"]
    g, p = dims["S5_G"], dims["S5_P"]
    t = S5_CHUNK
    m = b * l
    n_kt, kt, two_ns = lp["s5_bs"].shape
    y, hfin = pl.pallas_call(
        functools.partial(_s5_chunk_kernel, t_chunk=t),
        grid=(b, n_kt),
        in_specs=[pl.BlockSpec((l, kt), lambda bi, gi: (bi, off["u_s5"] // kt + gi)),
                  pl.BlockSpec((1, t * kt, kt), lambda bi, gi: (gi, 0, 0)),
                  pl.BlockSpec((1, t * kt, two_ns), lambda bi, gi: (gi, 0, 0)),
                  pl.BlockSpec((1, two_ns, t * kt), lambda bi, gi: (gi, 0, 0)),
                  pl.BlockSpec((1, 1, two_ns), lambda bi, gi: (gi, 0, 0)),
                  pl.BlockSpec((1, 1, two_ns), lambda bi, gi: (gi, 0, 0))],
        out_specs=[pl.BlockSpec((l, kt), lambda bi, gi: (bi, gi)),
                   pl.BlockSpec((1, 1, 1, two_ns), lambda bi, gi: (bi, gi, 0, 0))],
        out_shape=[jax.ShapeDtypeStruct((m, w), F32),
                   jax.ShapeDtypeStruct((b, n_kt, 1, two_ns), F32)],
        scratch_shapes=[pltpu.VMEM((l // t, t * kt), BF16)],
        compiler_params=_cparams("parallel", "parallel"),
        name="prompt_s5_chunks",
    )(proj, lp["s5_bdrev"], lp["s5_smat"], lp["s5_omat"], lp["s5_pr"], lp["s5_pi"])
    tm = min(ROW_TILE, l)
    br = pl.pallas_call(
        _glu_gate_kernel,
        grid=(m // tm,),
        in_specs=[pl.BlockSpec((tm, w), lambda i: (i, 0)),
                  pl.BlockSpec((w, w), lambda i: (0, 0)),
                  pl.BlockSpec((1, w), lambda i: (0, 0)),
                  pl.BlockSpec((tm, w), lambda i: (i, off["g_s5"] // w))],
        out_specs=pl.BlockSpec((tm, w), lambda i: (i, 0)),
        out_shape=jax.ShapeDtypeStruct((m, w), BF16),
        compiler_params=_cparams("parallel"),
        name="prompt_glu_gate",
    )(y, lp["w_glu"], lp["b_glu"], proj)
    ns = two_ns // 2
    hfin = hfin.reshape(b, n_kt, two_ns)
    return br, hfin[:, :, :ns].reshape(b, g, p), hfin[:, :, ns:].reshape(b, g, p)


def _pool_kernel(u_ref, halo_ref, g_ref, w_ref, sc_ref, o_ref, *, tiles_per_seq, halo):
    i = pl.program_id(0)
    x = u_ref[...]
    tm, width = x.shape
    gw = width // len(POOL_WINDOWS)
    first = (i % tiles_per_seq) == 0
    hal = jnp.where(first, 0.0, halo_ref[...])
    ext = jnp.concatenate([hal, x], axis=0)
    pos = lax.broadcasted_iota(jnp.int32, (tm, 1), 0) + (i % tiles_per_seq) * tm
    outs = []
    for gi, win in enumerate(POOL_WINDOWS):
        xg = x[:, gi * gw:(gi + 1) * gw]
        eg = ext[:, gi * gw:(gi + 1) * gw]
        s = xg
        for k in range(1, win):
            s = s + pltpu.roll(eg, k, axis=0)[halo:halo + tm]
        cnt = jnp.minimum(pos + 1, win).astype(F32)
        dlt = s / cnt - xg
        outs.append(_dot(dlt.astype(BF16), w_ref[gi]))
    y = jnp.concatenate(outs, axis=1) * sc_ref[...]
    o_ref[...] = (y * _silu(g_ref[...])).astype(BF16)


def _gmlp_kernel(u_ref, v_ref, g_ref, lng_ref, lnb_ref, ws_ref, bs_ref, o_ref, *, chunk):
    v = v_ref[...]
    tm, width = v.shape
    n_groups = ws_ref.shape[0]
    gw = width // n_groups
    mu = jnp.mean(v, axis=-1, keepdims=True)
    vc = v - mu
    vn = vc * lax.rsqrt(jnp.mean(vc * vc, axis=-1, keepdims=True) + NORM_EPS) * lng_ref[...] + lnb_ref[...]
    vn16 = vn.astype(BF16)
    bs = bs_ref[...]
    rows = []
    for c in range(tm // chunk):
        cols = []
        for gi in range(n_groups):
            blk = vn16[c * chunk:(c + 1) * chunk, gi * gw:(gi + 1) * gw]
            cols.append(_dot(ws_ref[gi], blk) + bs[:, gi:gi + 1])
        rows.append(jnp.concatenate(cols, axis=1))
    mixed = jnp.concatenate(rows, axis=0)
    o_ref[...] = (u_ref[...] * mixed * _silu(g_ref[...])).astype(BF16)


def prompt_pool(proj, lp, dims):
    b, l, w, off = dims["B"], dims["L"], dims["W"], dims["off"]
    m = b * l
    tm = min(ROW_TILE, l)
    halo = 16
    nl = l // tm
    n_g = len(POOL_WINDOWS)
    return pl.pallas_call(
        functools.partial(_pool_kernel, tiles_per_seq=nl, halo=halo),
        grid=(m // tm,),
        in_specs=[pl.BlockSpec((tm, w), lambda i: (i, off["u_pool"] // w)),
                  pl.BlockSpec((halo, w), lambda i: (jnp.maximum(i * (tm // halo) - 1, 0), off["u_pool"] // w)),
                  pl.BlockSpec((tm, w), lambda i: (i, off["g_pool"] // w)),
                  pl.BlockSpec((n_g, w // n_g, w // n_g), lambda i: (0, 0, 0)),
                  pl.BlockSpec((1, w), lambda i: (0, 0))],
        out_specs=pl.BlockSpec((tm, w), lambda i: (i, 0)),
        out_shape=jax.ShapeDtypeStruct((m, w), BF16),
        compiler_params=_cparams("parallel"),
        name="prompt_pool",
    )(proj, proj, proj, lp["pool_w"], lp["pool_scale"])


def prompt_gmlp(proj, lp, dims):
    b, l, w, off = dims["B"], dims["L"], dims["W"], dims["off"]
    m = b * l
    chunk = dims["CHUNK"]
    tm = min(ROW_TILE, l)
    n_g = lp["gmlp_ws"].shape[0]
    return pl.pallas_call(
        functools.partial(_gmlp_kernel, chunk=chunk),
        grid=(m // tm,),
        in_specs=[pl.BlockSpec((tm, w), lambda i: (i, off["u_gm"] // w)),
                  pl.BlockSpec((tm, w), lambda i: (i, off["v_gm"] // w)),
                  pl.BlockSpec((tm, w), lambda i: (i, off["g_gm"] // w)),
                  pl.BlockSpec((1, w), lambda i: (0, 0)),
                  pl.BlockSpec((1, w), lambda i: (0, 0)),
                  pl.BlockSpec((n_g, chunk, chunk), lambda i: (0, 0, 0)),
                  pl.BlockSpec((chunk, n_g), lambda i: (0, 0))],
        out_specs=pl.BlockSpec((tm, w), lambda i: (i, 0)),
        out_shape=jax.ShapeDtypeStruct((m, w), BF16),
        compiler_params=_cparams("parallel"),
        name="prompt_gmlp",
    )(proj, proj, proj, lp["ln_g"], lp["ln_b"], lp["gmlp_ws"], lp["gmlp_bs_t"])


def _sample_qkv_kernel(p_ref, xn_ref, wkpe_ref, qg_ref, kg_ref, wuq_ref, wukt_ref, c_ref, s1_ref, s2_ref,
                       klat_ref, kpe_ref, qlat_ref, qpe_ref, *, n_heads, d_nope, r_q, r_kv, d_rope, scale):
    pr = p_ref[...]
    c, s1, s2 = c_ref[...], s1_ref[...], s2_ref[...]
    klat_ref[...] = _rms(pr[:, r_q:r_q + r_kv], kg_ref[...])
    kpe = _rope128(_dot(xn_ref[...], wkpe_ref[...]), c, s1, s2)
    kpe_ref[...] = kpe[:, :d_rope]
    cq = _rms(pr[:, :r_q], qg_ref[...]).astype(BF16)
    q = _dot(cq, wuq_ref[...])
    dq = d_nope + LANES
    for h in range(n_heads):
        qn = q[:, h * dq:h * dq + d_nope].astype(BF16)
        qlat_ref[h] = (_dot(qn, wukt_ref[h]) * scale).astype(BF16)
        qpe_ref[h] = (_rope128(q[:, h * dq + d_nope:(h + 1) * dq], c, s1, s2) * scale).astype(BF16)


def sample_qkv(proj, xn, lp, dims, rope_row):
    ms, h = dims["MS"], dims["H"]
    d_nope, d_rope, r_q, r_kv, d = dims["QK_NOPE"], dims["QK_ROPE"], dims["Q_LORA"], dims["KV_LORA"], dims["D"]
    dk = d_nope + LANES
    full = lambda *shape: pl.BlockSpec(shape, lambda i: (0,) * len(shape))
    return pl.pallas_call(
        functools.partial(_sample_qkv_kernel, n_heads=h, d_nope=d_nope, r_q=r_q, r_kv=r_kv,
                          d_rope=d_rope, scale=dims["scale"]),
        grid=(1,),
        in_specs=[pl.BlockSpec((ms, r_q + r_kv), lambda i: (0, 0)),
                  full(ms, d), full(d, LANES), full(1, r_q), full(1, r_kv), full(r_q, h * dk),
                  full(h, d_nope, r_kv), full(1, LANES), full(1, LANES), full(1, LANES)],
        out_specs=[full(ms, r_kv), full(ms, d_rope), full(h, ms, r_kv), full(h, ms, LANES)],
        out_shape=[jax.ShapeDtypeStruct((ms, r_kv), F32), jax.ShapeDtypeStruct((ms, d_rope), F32),
                   jax.ShapeDtypeStruct((h, ms, r_kv), BF16), jax.ShapeDtypeStruct((h, ms, LANES), BF16)],
        compiler_params=_cparams("arbitrary"),
        name="sample_qkv",
    )(proj, xn, lp["w_kpe"], lp["q_norm"], lp["kv_norm"], lp["w_uq_pad"], lp["w_uk_t"], *rope_row)


def _decode_kernel(pt_ref, qlat_ref, qpe_ref, knl_ref, knp_ref, ckv_hbm, ckr_hbm, o_ref,
                   kvbuf, krbuf, sems, m_sc, l_sc, acc_sc, *, layer, pages_per_chunk, n_chunks, page, d_rope):
    b = pl.program_id(0)
    c = pl.program_id(1)
    step = b * n_chunks + c
    n_steps = pl.num_programs(0) * n_chunks
    slot = step % 2

    def copies(bb, cc, sl):
        out = []
        for pi in range(pages_per_chunk):
            pid = pt_ref[bb, cc * pages_per_chunk + pi]
            out.append(pltpu.make_async_copy(ckv_hbm.at[layer, pid],
                                             kvbuf.at[sl, pl.ds(pi * page, page)], sems.at[0, sl]))
            out.append(pltpu.make_async_copy(ckr_hbm.at[layer, pid], krbuf.at[sl, pi], sems.at[1, sl]))
        return out

    @pl.when(step == 0)
    def _():
        for cp in copies(b, c, slot):
            cp.start()

    @pl.when(step + 1 < n_steps)
    def _():
        nxt = step + 1
        for cp in copies(nxt // n_chunks, nxt % n_chunks, 1 - slot):
            cp.start()

    for cp in copies(b, c, slot):
        cp.wait()

    @pl.when(c == 0)
    def _():
        m_sc[...] = jnp.full(m_sc.shape, -jnp.inf, F32)
        l_sc[...] = jnp.zeros(l_sc.shape, F32)
        acc_sc[...] = jnp.zeros(acc_sc.shape, F32)

    qlat = qlat_ref[0]
    qpe = qpe_ref[0][:, :d_rope]
    kv16 = kvbuf[slot].astype(BF16)
    s_pe = jnp.concatenate([_dot(qpe, krbuf[slot, pi].astype(BF16)) for pi in range(pages_per_chunk)], axis=1)
    s = _dot_nt(qlat, kv16) + s_pe
    m_old = m_sc[...]
    m_new = jnp.maximum(m_old, jnp.max(s, axis=-1, keepdims=True))
    alpha = jnp.exp(m_old - m_new)
    p = jnp.exp(s - m_new)
    l_new = alpha * l_sc[...] + jnp.sum(p, axis=-1, keepdims=True)
    acc_new = alpha * acc_sc[...] + _dot(p.astype(BF16), kv16)
    m_sc[...] = m_new
    l_sc[...] = l_new
    acc_sc[...] = acc_new

    @pl.when(c == n_chunks - 1)
    def _():
        knl = knl_ref[0]
        s_new = (jnp.sum(qlat.astype(F32) * knl, axis=-1, keepdims=True)
                 + jnp.sum(qpe.astype(F32) * knp_ref[0], axis=-1, keepdims=True))
        m_fin = jnp.maximum(m_new, s_new)
        a_old = jnp.exp(m_new - m_fin)
        p_new = jnp.exp(s_new - m_fin)
        o_ref[0] = (a_old * acc_new + p_new * knl) / (a_old * l_new + p_new)


def sample_decode(page_table, qlat, qpe, k_new_lat, k_new_pe, cache_kv, cache_kr, layer, dims):
    ms, h, r_kv, d_rope = dims["MS"], dims["H"], dims["KV_LORA"], dims["QK_ROPE"]
    n_pages = page_table.shape[1]
    page = cache_kv.shape[2]
    ppc = min(DECODE_PAGES, n_pages)
    n_chunks = n_pages // ppc
    grid_spec = pltpu.PrefetchScalarGridSpec(
        num_scalar_prefetch=1,
        grid=(ms, n_chunks),
        in_specs=[pl.BlockSpec((1, h, r_kv), lambda b, c, pt: (b, 0, 0)),
                  pl.BlockSpec((1, h, LANES), lambda b, c, pt: (b, 0, 0)),
                  pl.BlockSpec((1, 1, r_kv), lambda b, c, pt: (b, 0, 0)),
                  pl.BlockSpec((1, 1, d_rope), lambda b, c, pt: (b, 0, 0)),
                  pl.BlockSpec(memory_space=pl.ANY),
                  pl.BlockSpec(memory_space=pl.ANY)],
        out_specs=pl.BlockSpec((1, h, r_kv), lambda b, c, pt: (b, 0, 0)),
        scratch_shapes=[pltpu.VMEM((2, ppc * page, r_kv), F32),
                        pltpu.VMEM((2, ppc, d_rope, page), F32),
                        pltpu.SemaphoreType.DMA((2, 2)),
                        pltpu.VMEM((h, 1), F32), pltpu.VMEM((h, 1), F32), pltpu.VMEM((h, r_kv), F32)])
    return pl.pallas_call(
        functools.partial(_decode_kernel, layer=layer, pages_per_chunk=ppc, n_chunks=n_chunks,
                          page=page, d_rope=d_rope),
        grid_spec=grid_spec,
        out_shape=jax.ShapeDtypeStruct((ms, h, r_kv), F32),
        compiler_params=_cparams("arbitrary", "arbitrary"),
        name="sample_decode",
    )(page_table, qlat, qpe, k_new_lat.reshape(ms, 1, r_kv), k_new_pe.reshape(ms, 1, d_rope), cache_kv, cache_kr)


def _sample_s5_kernel(u_ref, g_ref, h0r_ref, h0i_ref, ar_ref, ai_ref, bs_ref, cs_ref, d_ref, wglu_ref, bglu_ref,
                      br_ref, hr_ref, hi_ref):
    u = u_ref[...]
    u16 = u.astype(BF16)
    n_kt, kt, two_ns = bs_ref.shape
    ns = two_ns // 2
    ys = []
    for t in range(n_kt):
        bu = _dot(u16[:, t * kt:(t + 1) * kt], bs_ref[t])
        sl = slice(t * ns, (t + 1) * ns)
        ar, ai, h0r, h0i = ar_ref[:, sl], ai_ref[:, sl], h0r_ref[:, sl], h0i_ref[:, sl]
        hr = ar * h0r - ai * h0i + bu[:, :ns]
        hi = ar * h0i + ai * h0r + bu[:, ns:]
        hr_ref[:, sl] = hr
        hi_ref[:, sl] = hi
        hcat = jnp.concatenate([hr, hi], axis=1).astype(BF16)
        ys.append(_dot(hcat, cs_ref[t]))
    y = jnp.concatenate(ys, axis=1) + d_ref[...] * u
    z = _gelu_tanh(y)
    o = z * _sigmoid(_dot(z.astype(BF16), wglu_ref[...]) + bglu_ref[...])
    br_ref[...] = (o * _silu(g_ref[...])).astype(BF16)


def sample_s5(proj, h0r, h0i, lp, dims):
    ms, w, off = dims["MS"], dims["W"], dims["off"]
    n_state = h0r.shape[1]
    tr = min(64, ms)
    n_kt, kt, two_ns = lp["s5_bs"].shape
    row = lambda width, cb=0: pl.BlockSpec((tr, width), lambda i: (i, cb))
    full = lambda *shape: pl.BlockSpec(shape, lambda i: (0,) * len(shape))
    return pl.pallas_call(
        _sample_s5_kernel,
        grid=(ms // tr,),
        in_specs=[row(w, off["u_s5"] // w), row(w, off["g_s5"] // w), row(n_state), row(n_state),
                  full(1, n_state), full(1, n_state), full(n_kt, kt, two_ns), full(n_kt, two_ns, kt),
                  full(1, w), full(w, w), full(1, w)],
        out_specs=[row(w), row(n_state), row(n_state)],
        out_shape=[jax.ShapeDtypeStruct((ms, w), BF16), jax.ShapeDtypeStruct((ms, n_state), F32),
                   jax.ShapeDtypeStruct((ms, n_state), F32)],
        compiler_params=_cparams("parallel"),
        name="sample_s5",
    )(proj, proj, h0r, h0i, lp["s5_a_re"], lp["s5_a_im"], lp["s5_bs"], lp["s5_cs"], lp["s5_d"],
      lp["w_glu"], lp["b_glu"])


def _sample_mix_kernel(olat_ref, wuv_ref, gmla_ref, up_ref, gp_ref, past_ref, pw_ref, psc_ref,
                       ug_ref, vg_ref, gg_ref, lng_ref, lnb_ref, ws0_ref, bs0_ref,
                       bmla_ref, bpool_ref, bgm_ref, vrow_ref, *, n_heads, d_v, pool_cnt):
    gm = gmla_ref[...]
    for h in range(n_heads):
        o = _dot(olat_ref[h].astype(BF16), wuv_ref[h])
        bmla_ref[:, h * d_v:(h + 1) * d_v] = (o * _silu(gm[:, h * d_v:(h + 1) * d_v])).astype(BF16)

    u = up_ref[...]
    n_past = past_ref.shape[0]
    gw = u.shape[1] // len(POOL_WINDOWS)
    outs = []
    for gi, win in enumerate(POOL_WINDOWS):
        sl = slice(gi * gw, (gi + 1) * gw)
        s = u[:, sl]
        for k in range(1, win):
            s = s + past_ref[n_past - k][:, sl]
        dlt = s / pool_cnt[gi] - u[:, sl]
        outs.append(_dot(dlt.astype(BF16), pw_ref[gi]))
    yp = jnp.concatenate(outs, axis=1) * psc_ref[...]
    bpool_ref[...] = (yp * _silu(gp_ref[...])).astype(BF16)

    v = vg_ref[...]
    mu = jnp.mean(v, axis=-1, keepdims=True)
    vc = v - mu
    vn = vc * lax.rsqrt(jnp.mean(vc * vc, axis=-1, keepdims=True) + NORM_EPS) * lng_ref[...] + lnb_ref[...]
    vrow_ref[...] = vn
    mixed = ws0_ref[...] * vn + bs0_ref[...]
    bgm_ref[...] = (ug_ref[...] * mixed * _silu(gg_ref[...])).astype(BF16)


def sample_mix(proj, olat_hm, past_t, lp, dims):
    ms, w, h, d_v, r_kv, off = dims["MS"], dims["W"], dims["H"], dims["V_HEAD"], dims["KV_LORA"], dims["off"]
    tr = min(64, ms)
    n_past = past_t.shape[0]
    n_g = len(POOL_WINDOWS)
    pool_cnt = tuple(float(min(dims["N_PAST"] + 1, win)) for win in POOL_WINDOWS)
    row = lambda key: pl.BlockSpec((tr, w), lambda i: (i, off[key] // w))
    full = lambda *shape: pl.BlockSpec(shape, lambda i: (0,) * len(shape))
    out_row = pl.BlockSpec((tr, w), lambda i: (i, 0))
    return pl.pallas_call(
        functools.partial(_sample_mix_kernel, n_heads=h, d_v=d_v, pool_cnt=pool_cnt),
        grid=(ms // tr,),
        in_specs=[pl.BlockSpec((h, tr, r_kv), lambda i: (0, i, 0)), full(h, r_kv, d_v), row("g_mla"),
                  row("u_pool"), row("g_pool"), pl.BlockSpec((n_past, tr, w), lambda i: (0, i, 0)),
                  full(n_g, w // n_g, w // n_g), full(1, w),
                  row("u_gm"), row("v_gm"), row("g_gm"), full(1, w), full(1, w), full(1, w), full(1, w)],
        out_specs=[out_row, out_row, out_row, out_row],
        out_shape=[jax.ShapeDtypeStruct((ms, w), BF16), jax.ShapeDtypeStruct((ms, w), BF16),
                   jax.ShapeDtypeStruct((ms, w), BF16), jax.ShapeDtypeStruct((ms, w), F32)],
        compiler_params=_cparams("parallel"),
        name="sample_mix",
    )(olat_hm, lp["w_uv_h"], proj, proj, proj, past_t, lp["pool_w"], lp["pool_scale"],
      proj, proj, proj, lp["ln_g"], lp["ln_b"], lp["gmlp_ws0"], lp["gmlp_bs0"])


def _s5_params(lam_re, lam_im, log_dt, b_re, b_im, c_re, c_im, d):
    hp = lax.Precision.HIGHEST
    g, p, gc = b_re.shape
    t = S5_CHUNK
    dt = jnp.exp(log_dt)[:, None]
    ld_re, ld_im = lam_re * dt, lam_im * dt
    mag = jnp.exp(ld_re)
    a_re, a_im = mag * jnp.cos(ld_im), mag * jnp.sin(ld_im)
    den = lam_re * lam_re + lam_im * lam_im
    num_re, num_im = a_re - 1.0, a_im
    coef_re = (num_re * lam_re + num_im * lam_im) / den
    coef_im = (num_im * lam_re - num_re * lam_im) / den
    bb_re = coef_re[..., None] * b_re - coef_im[..., None] * b_im
    bb_im = coef_re[..., None] * b_im + coef_im[..., None] * b_re

    pw_re, pw_im = [jnp.ones_like(a_re)], [jnp.zeros_like(a_im)]
    for _ in range(t):
        r, i = pw_re[-1], pw_im[-1]
        pw_re.append(r * a_re - i * a_im)
        pw_im.append(r * a_im + i * a_re)
    pw_re, pw_im = jnp.stack(pw_re), jnp.stack(pw_im)

    e_re = pw_re[:t, ..., None] * bb_re - pw_im[:t, ..., None] * bb_im
    e_im = pw_re[:t, ..., None] * bb_im + pw_im[:t, ..., None] * bb_re

    cb = (jnp.einsum("gcp,kgpd->kgdc", c_re, e_re, precision=hp)
          - jnp.einsum("gcp,kgpd->kgdc", c_im, e_im, precision=hp))
    cb = cb.at[0].add(jnp.eye(gc, dtype=F32)[None] * d[:, None, :])

    kt = LANES
    gpt = kt // gc
    n_kt = g // gpt
    eye = jnp.eye(gpt, dtype=F32)
    tile_rows = jnp.arange(kt)[:, None] // gc
    tile_cols = jnp.arange(kt)[None, :] // gc
    bd = jnp.where(tile_rows == tile_cols, jnp.tile(cb.reshape(t, n_kt, kt, gc), (1, 1, 1, gpt)), 0.0)
    bd = bd.transpose(1, 0, 2, 3)

    def packed(x_re, x_im):
        r = x_re.reshape(t + 1, n_kt, gpt * p).transpose(1, 0, 2)
        i = x_im.reshape(t + 1, n_kt, gpt * p).transpose(1, 0, 2)
        return jnp.concatenate([r, r], axis=2), jnp.concatenate([-i, i], axis=2)

    pr, pi = packed(pw_re, pw_im)

    def bdiag_in(bb):
        x = bb.reshape(n_kt, gpt, p, gc)
        return jnp.einsum("tgpc,gh->tgchp", x, eye).reshape(n_kt, kt, gpt * p)

    def bdiag_out(cc):
        x = cc.reshape(n_kt, gpt, gc, p)
        return jnp.einsum("tgcp,gh->tgphc", x, eye).reshape(n_kt, gpt * p, kt)

    bs = jnp.concatenate([bdiag_in(bb_re), bdiag_in(bb_im)], axis=2)
    cs = jnp.concatenate([bdiag_out(c_re), -bdiag_out(c_im)], axis=1)
    smat = jnp.concatenate([jnp.concatenate([bdiag_in(e_re[t - 1 - i]), bdiag_in(e_im[t - 1 - i])], axis=2)
                            for i in range(t)], axis=1)
    f_re = c_re[None] * pw_re[1:, :, None, :] - c_im[None] * pw_im[1:, :, None, :]
    f_im = c_re[None] * pw_im[1:, :, None, :] + c_im[None] * pw_re[1:, :, None, :]
    omat = jnp.concatenate([jnp.concatenate([bdiag_out(f_re[j]), -bdiag_out(f_im[j])], axis=1)
                            for j in range(t)], axis=2)
    bdrev = bd[:, ::-1].reshape(n_kt, t * kt, kt)
    pr, pi = pr[:, t:t + 1], pi[:, t:t + 1]
    return {"s5_bdrev": bdrev.astype(BF16), "s5_smat": smat.astype(BF16), "s5_omat": omat.astype(BF16),
            "s5_pr": pr, "s5_pi": pi,
            "s5_a_re": a_re.reshape(1, g * p), "s5_a_im": a_im.reshape(1, g * p),
            "s5_bs": bs.astype(BF16), "s5_cs": cs.astype(BF16), "s5_d": d.reshape(1, g * gc)}


def _rope_tables(pos, d_rope):
    half = d_rope // 2
    freqs = ROPE_THETA ** (-jnp.arange(half, dtype=F32) / half)
    ang = pos.astype(F32)[:, None] * freqs[None, :]
    cos, sin = jnp.cos(ang), jnp.sin(ang)
    z = jnp.zeros((pos.shape[0], LANES - d_rope), F32)
    zh = jnp.zeros_like(cos)
    return (jnp.concatenate([cos, cos, z], axis=1),
            jnp.concatenate([-sin, zh, z], axis=1),
            jnp.concatenate([zh, sin, z], axis=1))


def _layer_params(l, dims, w_in, norm_attn, mla_q_norm, mla_kv_norm, mla_w_uq, mla_w_uk, mla_w_uv,
                  s5, s5_w_glu, s5_b_glu, pool_w, pool_scale, gmlp_ln_g, gmlp_ln_b, gmlp_w_s, gmlp_b_s,
                  w_branch, w_out):
    r_q, r_kv, d_rope, d_nope, h, w = (dims["Q_LORA"], dims["KV_LORA"], dims["QK_ROPE"], dims["QK_NOPE"],
                                       dims["H"], dims["W"])
    lp = {"norm": norm_attn[l][None, :],
          "q_norm": mla_q_norm[l][None, :], "kv_norm": mla_kv_norm[l][None, :]}
    lp["w_in"], lp["w_kpe"] = pack_w_in(w_in, l, r_q + r_kv, d_rope, INPROJ_COLS)
    wuq = mla_w_uq[l]
    lp["w_uq_pad"] = jnp.pad(wuq, ((0, 0), (0, 0), (0, LANES - d_rope))).reshape(r_q, -1).astype(BF16)
    lp["w_uk_flat"] = mla_w_uk[l].reshape(r_kv, -1).astype(BF16)
    lp["w_uv_flat"] = mla_w_uv[l].reshape(r_kv, -1).astype(BF16)
    lp["w_uk_t"] = mla_w_uk[l].transpose(1, 2, 0).astype(BF16)
    lp["w_uv_h"] = mla_w_uv[l].transpose(1, 0, 2).astype(BF16)
    lp.update(s5)
    lp["w_glu"] = s5_w_glu[l].astype(BF16)
    lp["b_glu"] = s5_b_glu[l][None, :]
    lp["pool_w"] = pool_w[l].astype(BF16)
    lp["pool_scale"] = pool_scale[l][None, :]
    lp["ln_g"] = gmlp_ln_g[l][None, :]
    lp["ln_b"] = gmlp_ln_b[l][None, :]
    chunk = gmlp_w_s.shape[2]
    ws = gmlp_w_s[l] * jnp.tril(jnp.ones((chunk, chunk), F32))[None]
    n_g = ws.shape[0]
    lp["gmlp_ws"] = ws.astype(BF16)
    lp["gmlp_bs_t"] = gmlp_b_s[l].T
    lp["gmlp_ws0"] = jnp.repeat(ws[:, 0, 0], w // n_g)[None, :]
    lp["gmlp_bs0"] = jnp.repeat(gmlp_b_s[l][:, 0], w // n_g)[None, :]
    lp["w_branch"] = w_branch[l].astype(BF16)
    lp["w_out"] = w_out[l].astype(BF16)
    return lp


def kernel(x_prompt, x_sample, cache_kv_latent, cache_k_rope, state_s5_re, state_s5_im, state_pool, page_table,
           norm_attn, w_in, mla_q_norm, mla_kv_norm, mla_w_uq, mla_w_uk, mla_w_uv, s5_lambda_re, s5_lambda_im,
           s5_log_dt, s5_b_re, s5_b_im, s5_c_re, s5_c_im, s5_d, s5_w_glu, s5_b_glu, pool_w, pool_scale,
           gmlp_ln_g, gmlp_ln_b, gmlp_w_s, gmlp_b_s, w_branch, w_out, norm_final):
    b, l, d = x_prompt.shape
    ms, dec_seq, _ = x_sample.shape
    assert dec_seq == 1, "one new token per sampled sequence"
    depth = w_in.shape[0]
    w = w_branch.shape[2]
    n_branch = w_branch.shape[1]
    r_q, r_kv = mla_q_norm.shape[1], mla_kv_norm.shape[1]
    h, d_nope, d_v = mla_w_uk.shape[2], mla_w_uk.shape[3], mla_w_uv.shape[3]
    d_rope = mla_w_uq.shape[3] - d_nope
    n_past = page_table.shape[1] * cache_kv_latent.shape[2]
    pool_buf = state_pool.shape[2]
    assert d_nope == LANES and d_v == LANES and d_rope <= LANES and pool_buf == max(POOL_WINDOWS) - 1

    names = ("c_q", "c_kv", "g_mla", "u_s5", "g_s5", "u_pool", "g_pool", "u_gm", "v_gm", "g_gm", "g_merge")
    widths = (r_q, r_kv, w, w, w, w, w, w, w, w, n_branch * d)
    off, acc = {}, 0
    for nm, wd in zip(names, widths):
        off[nm] = acc
        acc += wd
    dims = {"B": b, "L": l, "D": d, "MS": ms, "W": w, "H": h, "QK_NOPE": d_nope, "QK_ROPE": d_rope,
            "V_HEAD": d_v, "Q_LORA": r_q, "KV_LORA": r_kv, "S5_G": s5_b_re.shape[1], "S5_P": s5_b_re.shape[2],
            "S5_GC": s5_b_re.shape[3], "CHUNK": gmlp_w_s.shape[2], "N_PAST": n_past, "off": off,
            "scale": float((d_nope + d_rope) ** -0.5)}

    rope_p = _rope_tables(jnp.arange(l, dtype=jnp.int32), d_rope)
    rope_s = _rope_tables(jnp.full((1,), n_past, dtype=jnp.int32), d_rope)
    final_gain = norm_final[None, :]
    cache_kr_t = cache_k_rope.transpose(0, 1, 3, 2)
    w_in_t = w_in.transpose(0, 2, 1)

    mp = b * l
    hp = x_prompt.reshape(mp, d)
    hs = x_sample.reshape(ms, d)
    tm_p = min(INPROJ_TILE, l)
    tn_in = INPROJ_COLS
    new_p, new_s = [], []
    for layer in range(depth):
        s5 = _s5_params(s5_lambda_re[layer], s5_lambda_im[layer], s5_log_dt[layer], s5_b_re[layer],
                        s5_b_im[layer], s5_c_re[layer], s5_c_im[layer], s5_d[layer])
        lp = _layer_params(layer, dims, w_in_t, norm_attn, mla_q_norm, mla_kv_norm, mla_w_uq, mla_w_uk, mla_w_uv,
                           s5, s5_w_glu, s5_b_glu, pool_w, pool_scale, gmlp_ln_g, gmlp_ln_b, gmlp_w_s, gmlp_b_s,
                           w_branch, w_out)
        last = layer == depth - 1

        proj, gates, xn = norm_inproj(hp, lp["norm"], lp["w_in"], off["g_merge"], tm_p, tn_in)
        br_mla, klat, kpe = prompt_mla(proj, xn, lp, dims, rope_p)
        br_s5, s5r, s5i = prompt_s5(proj, lp, dims)
        br_pool = prompt_pool(proj, lp, dims)
        br_gm = prompt_gmlp(proj, lp, dims)
        merged = merge_branches([br_mla, br_s5, br_pool, br_gm], gates, lp["w_branch"], tm_p, 512)
        pool_tail = proj.reshape(b, l, -1)[:, l - pool_buf:, off["u_pool"]:off["u_pool"] + w]
        new_p.append((klat.reshape(b, l, r_kv), kpe.reshape(b, l, d_rope), s5r, s5i, pool_tail))
        hp = out_proj(hp, merged, lp["w_out"], min(ROW_TILE, l), final_gain if last else None)

        proj_s, gates_s, xn_s = norm_inproj(hs, lp["norm"], lp["w_in"], off["g_merge"], ms, tn_in)
        klat_s, kpe_s, qlat_hm, qpe_hm = sample_qkv(proj_s, xn_s, lp, dims, rope_s)
        olat = sample_decode(page_table, qlat_hm.transpose(1, 0, 2), qpe_hm.transpose(1, 0, 2), klat_s, kpe_s,
                             cache_kv_latent, cache_kr_t, layer, dims)
        g_s, p_s = state_s5_re.shape[2], state_s5_re.shape[3]
        br_s5_s, s5r_s, s5i_s = sample_s5(proj_s, state_s5_re[layer].reshape(ms, g_s * p_s),
                                          state_s5_im[layer].reshape(ms, g_s * p_s), lp, dims)
        past = state_pool[layer]
        br_mla_s, br_pool_s, br_gm_s, v_rows = sample_mix(proj_s, olat.transpose(1, 0, 2),
                                                          past.transpose(1, 0, 2), lp, dims)
        merged_s = merge_branches([br_mla_s, br_s5_s, br_pool_s, br_gm_s], gates_s, lp["w_branch"], ms, 512)
        u_pool_s = proj_s[:, off["u_pool"]:off["u_pool"] + w]
        new_s.append((klat_s.reshape(ms, 1, r_kv), kpe_s.reshape(ms, 1, d_rope),
                      s5r_s.reshape(ms, g_s, p_s), s5i_s.reshape(ms, g_s, p_s),
                      jnp.concatenate([past[:, 1:], u_pool_s[:, None, :]], axis=1),
                      v_rows.reshape(ms, 1, w)))
        hs = out_proj(hs, merged_s, lp["w_out"], ms, final_gain if last else None)

    stack = lambda states, i: jnp.stack([s[i] for s in states], axis=0)
    return (hp.reshape(b, l, d), hs.reshape(ms, 1, d),
            stack(new_p, 0), stack(new_p, 1), stack(new_p, 2), stack(new_p, 3), stack(new_p, 4),
            stack(new_s, 0), stack(new_s, 1), stack(new_s, 2), stack(new_s, 3), stack(new_s, 4),
            stack(new_s, 5))
```
